```python
import jax
import jax.numpy as jnp
from jax import lax
import numpy as np

D_MODEL = 2048
BATCH = 32
SEQ = 256
DEPTH = 4
DEC_BATCH = 4
DEC_SEQ = 2048
PAST_LEN = 256

GRID_W = 64
N_EVEN = (DEPTH + 1) // 2
N_ODD = DEPTH // 2
CHUNK = 128
EPS = 1e-6
ROPE_BASE = 10000.0
H_A = D_MODEL // 512
DK_A = 256
DV_A = 256
H_B = D_MODEL // 256
DK_B = 128
DV_B = 128
K_CONV = 5
H_C = D_MODEL // 128
DH_C = 128
D_C = H_C * DH_C
KH_MAX = 8
KW = 16
A_QK = H_A * DK_A
A_V = H_A * DV_A
B_QK = H_B * DK_B
B_V = H_B * DV_B
B_QKV = 2 * B_QK + B_V
EVEN_SPLITS = (A_QK, 2 * A_QK, 2 * A_QK + A_V, 2 * A_QK + 2 * A_V, 2 * A_QK + 2 * A_V + B_QKV, 2 * A_QK + 2 * A_V + B_QKV + B_V, 2 * A_QK + 2 * A_V + B_QKV + B_V + 2 * H_B)
EVEN_IN = EVEN_SPLITS[-1] + 2 * H_B
EVEN_MIX = A_V + B_V
N_GROUPS = 4
EXP_PER_GROUP = 8
N_EXPERTS = N_GROUPS * EXP_PER_GROUP
TOP_K = 2
D_EXPERT = D_MODEL // 4
MOE_BLOCK = 128

kernel_name = 'hybrid_retnet_gdn_natten_hmoe_diffusion_step'


def rmsnorm(x, g):
    x32 = x.astype(jnp.float32)
    y = x32 * lax.rsqrt(jnp.mean(x32 * x32, axis=-1, keepdims=True) + EPS)
    return (y * g.astype(jnp.float32)).astype(x.dtype)


def adaln(cvec, w, b):
    m = jax.nn.silu(cvec) @ w + b
    return m.reshape(cvec.shape[0], 6, 1, D_MODEL)


def l2norm(x):
    return x * lax.rsqrt(jnp.sum(x * x, axis=-1, keepdims=True) + EPS)


def to_chunks(a):
    b, t = a.shape[:2]
    a = a.reshape((b, t // CHUNK, CHUNK) + a.shape[2:])
    return jnp.moveaxis(a, 2, 3)


def axial_rope(x):
    b, t, h, dh = x.shape
    nf = dh // 4
    pos = jnp.arange(t)
    pos = jnp.stack([pos // GRID_W, pos % GRID_W], axis=-1).astype(jnp.float32)
    inv = ROPE_BASE ** (-jnp.arange(nf, dtype=jnp.float32) / nf)
    ang = pos[:, :, None] * inv
    cos = jnp.cos(ang)[None, :, None]
    sin = jnp.sin(ang)[None, :, None]
    xr = x.reshape(b, t, h, 2, 2, nf)
    x1, x2 = xr[..., 0, :], xr[..., 1, :]
    out = jnp.stack([x1 * cos - x2 * sin, x2 * cos + x1 * sin], axis=-2)
    return out.reshape(b, t, h, dh)


def retention_scan(q, k, v, log_gamma, s0):
    b, t, h, _ = q.shape
    dv = v.shape[-1]
    qc, kc, vc = to_chunks(q), to_chunks(k), to_chunks(v)
    idx = jnp.arange(CHUNK, dtype=jnp.float32)
    diff = idx[:, None] - idx[None, :]
    dmask = jnp.where(diff >= 0, jnp.exp(log_gamma[:, None, None] * jnp.maximum(diff, 0.0)), 0.0)
    intra = jnp.einsum('bnhij,bnhje->bnhie', jnp.einsum('bnhid,bnhjd->bnhij', qc, kc) * dmask, vc)
    q_dec = jnp.exp(log_gamma[:, None] * (idx + 1.0))
    k_dec = jnp.exp(log_gamma[:, None] * (CHUNK - 1.0 - idx))
    c_dec = jnp.exp(log_gamma * CHUNK)
    kv = jnp.einsum('bnhjd,hj,bnhje->bnhde', kc, k_dec, vc)

    def step(s, kv_n):
        return s * c_dec[:, None, None] + kv_n, s

    s_fin, s_start = lax.scan(step, s0, jnp.moveaxis(kv, 1, 0))
    inter = jnp.einsum('bnhid,hi,nbhde->bnhie', qc, q_dec, s_start)
    o = jnp.transpose(intra + inter, (0, 1, 3, 2, 4)).reshape(b, t, h, dv)
    return o, s_fin


def gdn_scan(q, k, v, g, beta, s0):
    b, t, h, _ = q.shape
    dv = v.shape[-1]
    qc, kc, vc = to_chunks(q), to_chunks(k), to_chunks(v)
    gc = jnp.cumsum(to_chunks(g), axis=-1)
    bc = to_chunks(beta)
    incl = jnp.tril(jnp.ones((CHUNK, CHUNK), dtype=bool))
    strict = jnp.tril(jnp.ones((CHUNK, CHUNK), dtype=bool), -1)
    diff = gc[..., :, None] - gc[..., None, :]
    decay = jnp.where(incl, jnp.exp(jnp.where(incl, diff, 0.0)), 0.0)
    kk = jnp.einsum('bnhid,bnhjd->bnhij', kc, kc)
    a_mat = jnp.eye(CHUNK, dtype=jnp.float32) + jnp.where(strict, kk * decay * bc[..., :, None], 0.0)
    rhs = jnp.concatenate([vc * bc[..., None], kc * (bc * jnp.exp(gc))[..., None]], axis=-1)
    sol = lax.linalg.triangular_solve(a_mat, rhs, left_side=True, lower=True, unit_diagonal=True)
    u, w = sol[..., :dv], sol[..., dv:]
    attn = jnp.einsum('bnhid,bnhjd->bnhij', qc, kc) * decay
    qg = qc * jnp.exp(gc)[..., None]
    kt = kc * jnp.exp(gc[..., -1:] - gc)[..., None]
    gl = jnp.exp(gc[..., -1])

    def step(s, xs):
        u_n, w_n, a_n, qg_n, kt_n, gl_n = xs
        v_new = u_n - jnp.einsum('bhcd,bhde->bhce', w_n, s)
        o = jnp.einsum('bhcd,bhde->bhce', qg_n, s) + jnp.einsum('bhij,bhje->bhie', a_n, v_new)
        s = s * gl_n[..., None, None] + jnp.einsum('bhcd,bhce->bhde', kt_n, v_new)
        return s, o

    xs = (jnp.moveaxis(u, 1, 0), jnp.moveaxis(w, 1, 0), jnp.moveaxis(attn, 1, 0), jnp.moveaxis(qg, 1, 0), jnp.moveaxis(kt, 1, 0), jnp.moveaxis(gl, 1, 0))
    s_fin, o = lax.scan(step, s0, xs)
    o = jnp.transpose(o, (1, 0, 3, 2, 4)).reshape(b, t, h, dv)
    return o, s_fin


def short_conv(x, w):
    pad = K_CONV // 2
    return lax.conv_general_dilated(x, w[:, None, :].astype(x.dtype), (1,), [(pad, pad)], dimension_numbers=('NWC', 'WIO', 'NWC'), feature_group_count=x.shape[-1])


def even_mixer(h, w_in, w_out, decay_logit, gn_w, conv_w, a_log, dt_bias, norm_w, s_ret, s_gdn, latent):
    f32 = jnp.float32
    b, t, _ = h.shape
    qa, ka, va, ga, qkv_b, zb, ab, bb = jnp.split(h @ w_in, list(EVEN_SPLITS), axis=-1)
    qa = qa.astype(f32).reshape(b, t, H_A, DK_A)
    ka = ka.astype(f32).reshape(b, t, H_A, DK_A)
    va = va.astype(f32).reshape(b, t, H_A, DV_A)
    if latent:
        qa, ka = axial_rope(qa), axial_rope(ka)
    qa = qa * DK_A ** -0.5
    log_gamma = jax.nn.log_sigmoid(decay_logit.astype(f32))
    s_ret = s_ret.astype(f32)
    o_f, sr_f = retention_scan(qa, ka, va, log_gamma[0], s_ret[:, 0])
    o_b, sr_b = retention_scan(qa[:, ::-1], ka[:, ::-1], va[:, ::-1], log_gamma[1], s_ret[:, 1])
    o = o_f + o_b[:, ::-1]
    o = o - jnp.mean(o, axis=-1, keepdims=True)
    o = o * lax.rsqrt(jnp.mean(o * o, axis=-1, keepdims=True) + EPS)
    out_a = jax.nn.silu(ga.astype(f32)) * (o.reshape(b, t, A_V) * gn_w.astype(f32))
    qkv = jax.nn.silu(short_conv(qkv_b, conv_w).astype(f32))
    qb, kb, vb = jnp.split(qkv, [B_QK, 2 * B_QK], axis=-1)
    qb = l2norm(qb.reshape(b, t, H_B, DK_B)) * DK_B ** -0.5
    kb = l2norm(kb.reshape(b, t, H_B, DK_B))
    vb = vb.reshape(b, t, H_B, DV_B)
    g = -jnp.exp(a_log.astype(f32)) * jax.nn.softplus(ab.astype(f32).reshape(b, t, 2, H_B) + dt_bias.astype(f32))
    beta = jax.nn.sigmoid(bb.astype(f32).reshape(b, t, 2, H_B))
    s_gdn = s_gdn.astype(f32)
    u_f, sg_f = gdn_scan(qb, kb, vb, g[:, :, 0], beta[:, :, 0], s_gdn[:, 0])
    u_b, sg_b = gdn_scan(qb[:, ::-1], kb[:, ::-1], vb[:, ::-1], g[:, ::-1, 1], beta[:, ::-1, 1], s_gdn[:, 1])
    u = u_f + u_b[:, ::-1]
    u = u * lax.rsqrt(jnp.mean(u * u, axis=-1, keepdims=True) + EPS) * norm_w.astype(f32)
    out_b = (u * jax.nn.silu(zb.astype(f32).reshape(b, t, H_B, DV_B))).reshape(b, t, B_V)
    y = jnp.concatenate([out_a, out_b], axis=-1).astype(h.dtype) @ w_out
    return y, jnp.stack([sr_f, sr_b], axis=1), jnp.stack([sg_f, sg_b], axis=1)


def na_project(h, w_in):
    b, t, _ = h.shape
    q, k, v = jnp.split((h @ w_in).astype(jnp.float32), 3, axis=-1)
    shp = (b, t, H_C, DH_C)
    return q.reshape(shp), k.reshape(shp), v.reshape(shp)


def context_attention(q, k, v):
    b, t, h, d = q.shape
    qb = jnp.moveaxis(q.reshape(b, t // CHUNK, CHUNK, h, d), 1, 0)

    def block(qi):
        s = jnp.einsum('bqhd,bkhd->bhqk', qi, k) * d ** -0.5
        return jnp.einsum('bhqk,bkhd->bqhd', jax.nn.softmax(s, axis=-1), v)

    o = lax.map(block, qb)
    return jnp.moveaxis(o, 0, 1).reshape(b, t, h, d)


def neighbourhood_attention(q, k, v, kc, vc, rpb):
    b, t, h, d = q.shape
    rows = t // GRID_W
    kh = min(KH_MAX, rows)
    scale = d ** -0.5
    qg = q.reshape(b, rows, GRID_W, h, d)
    kg = k.reshape(b, rows, GRID_W, h, d)
    vg = v.reshape(b, rows, GRID_W, h, d)
    col = jnp.arange(GRID_W)
    col_idx = jnp.clip(col - KW // 2, 0, GRID_W - KW)[:, None] + jnp.arange(KW)[None, :]
    dcol = col_idx - col[:, None] + (KW - 1)
    row_start = jnp.clip(jnp.arange(rows) - kh // 2, 0, rows - kh)
    rpb = rpb.astype(jnp.float32)

    def per_row(args):
        q_r, r = args
        r0 = row_start[r]
        k_band = lax.dynamic_slice_in_dim(kg, r0, kh, axis=1)
        v_band = lax.dynamic_slice_in_dim(vg, r0, kh, axis=1)
        k_win = k_band[:, :, col_idx]
        v_win = v_band[:, :, col_idx]
        drow = r0 + jnp.arange(kh) - r + (KH_MAX - 1)
        bias = rpb[:, drow[:, None, None], dcol[None]]
        s_win = jnp.einsum('bqhd,bkqwhd->bhqkw', q_r, k_win) * scale + jnp.transpose(bias, (0, 2, 1, 3))[None]
        s_ctx = jnp.einsum('bqhd,bphd->bhqp', q_r, kc) * scale
        p = jax.nn.softmax(jnp.concatenate([s_win.reshape(b, h, GRID_W, kh * KW), s_ctx], axis=-1), axis=-1)
        p_win = p[..., :kh * KW].reshape(b, h, GRID_W, kh, KW)
        return jnp.einsum('bhqkw,bkqwhd->bqhd', p_win, v_win) + jnp.einsum('bhqp,bphd->bqhd', p[..., kh * KW:], vc)

    o = lax.map(per_row, (jnp.moveaxis(qg, 1, 0), jnp.arange(rows)))
    return jnp.moveaxis(o, 0, 1).reshape(b, t, h, d)


def routed_experts(x, eid, wts, w1, w3, w2):
    n_tok, d = x.shape
    n_slot = eid.shape[0]
    order = jnp.argsort(eid)
    e_s = eid[order]
    tok_s = order // TOP_K
    w_s = wts[order]
    counts = jnp.bincount(eid, length=N_EXPERTS)
    padded = (counts + MOE_BLOCK - 1) // MOE_BLOCK * MOE_BLOCK
    ends_p = jnp.cumsum(padded)
    starts_p = ends_p - padded
    starts = jnp.cumsum(counts) - counts
    dest = starts_p[e_s] + jnp.arange(n_slot) - starts[e_s]
    n_blocks = -(-n_slot // MOE_BLOCK) + N_EXPERTS
    xp = jnp.zeros((n_blocks * MOE_BLOCK, d), x.dtype).at[dest].set(x[tok_s])
    blk_e = jnp.minimum(jnp.searchsorted(ends_p, jnp.arange(n_blocks) * MOE_BLOCK, side='right'), N_EXPERTS - 1)

    def expert_block(args):
        xb, e = args
        return (jax.nn.silu(xb @ w1[e]) * (xb @ w3[e])) @ w2[e]

    yp = lax.map(expert_block, (xp.reshape(n_blocks, MOE_BLOCK, d), blk_e)).reshape(-1, d)
    return jnp.zeros_like(x).at[tok_s].add(yp[dest] * w_s[:, None].astype(x.dtype))


def hier_moe(h, wg, bg, we, be, w1, w3, w2):
    b, t, d = h.shape
    x = h.reshape(b * t, d)
    lg = (x @ wg + bg).astype(jnp.float32)
    pg = jax.nn.softmax(lg, axis=-1)
    grp = jnp.argmax(lg, axis=-1)
    gate_g = jnp.take_along_axis(pg, grp[:, None], axis=1)
    le = (x @ we + be).astype(jnp.float32).reshape(-1, N_GROUPS, EXP_PER_GROUP)
    le = jnp.take_along_axis(le, grp[:, None, None], axis=1)[:, 0]
    top_v, top_i = lax.top_k(le, TOP_K)
    wts = gate_g * jax.nn.softmax(top_v, axis=-1)
    eid = (grp[:, None] * EXP_PER_GROUP + top_i).reshape(-1)
    return routed_experts(x, eid, wts.reshape(-1), w1, w3, w2).reshape(b, t, d)


def setup_inputs(seed: int = 0) -> dict:
    key = jax.random.key(seed)
    ks = jax.random.split(key, 32)
    f32 = jnp.float32
    D = D_MODEL

    def nrm(k, shape, scale):
        return jax.random.normal(k, shape, f32) * scale

    base_logit = jnp.asarray(np.log(2.0 ** (5 + np.arange(H_A)) - 1.0), dtype=f32)
    dt = jnp.exp(jax.random.uniform(ks[17], (N_EVEN, 2, H_B), f32, float(np.log(1e-3)), float(np.log(1e-1))))
    return {
        'x_prompt': nrm(ks[0], (BATCH, SEQ, D), 1.0),
        'x_sample': nrm(ks[1], (DEC_BATCH, DEC_SEQ, D), 1.0),
        'c': nrm(ks[2], (DEC_BATCH, D), 1.0),
        'state_ret': nrm(ks[3], (DEC_BATCH, N_EVEN, 2, H_A, DK_A, DV_A), 1.0),
        'state_gdn': nrm(ks[4], (DEC_BATCH, N_EVEN, 2, H_B, DK_B, DV_B), 0.1),
        'cache_k': nrm(ks[5], (DEC_BATCH, N_ODD, PAST_LEN, H_C, DH_C), 1.0),
        'cache_v': nrm(ks[6], (DEC_BATCH, N_ODD, PAST_LEN, H_C, DH_C), 1.0),
        'c_ctx': nrm(ks[7], (D,), 1.0),
        'w_mod': nrm(ks[8], (DEPTH, D, 6 * D), 0.5 * D ** -0.5),
        'b_mod': nrm(ks[9], (DEPTH, 6 * D), 0.02),
        'norm_g': 1.0 + nrm(ks[10], (DEPTH, 2, D), 0.02),
        'even_w_in': nrm(ks[11], (N_EVEN, D, EVEN_IN), D ** -0.5),
        'even_w_out': nrm(ks[12], (N_EVEN, EVEN_MIX, D), EVEN_MIX ** -0.5),
        'ret_decay_logit': base_logit[None, None, :] + nrm(ks[13], (N_EVEN, 2, H_A), 0.1),
        'ret_gn_w': 1.0 + nrm(ks[14], (N_EVEN, A_V), 0.02),
        'gdn_conv_w': nrm(ks[15], (N_EVEN, K_CONV, B_QKV), K_CONV ** -0.5),
        'gdn_a_log': jnp.log(jax.random.uniform(ks[16], (N_EVEN, 2, H_B), f32, 1.0, 16.0)),
        'gdn_dt_bias': dt + jnp.log(-jnp.expm1(-dt)),
        'gdn_norm_w': 1.0 + nrm(ks[18], (N_EVEN, DV_B), 0.02),
        'na_w_in': nrm(ks[19], (N_ODD, D, 3 * D_C), D ** -0.5),
        'na_w_out': nrm(ks[20], (N_ODD, D_C, D), D_C ** -0.5),
        'na_rpb': nrm(ks[21], (N_ODD, H_C, 2 * KH_MAX - 1, 2 * KW - 1), 0.1),
        'moe_wg': nrm(ks[22], (DEPTH, D, N_GROUPS), D ** -0.5),
        'moe_bg': nrm(ks[23], (DEPTH, N_GROUPS), 0.01),
        'moe_we': nrm(ks[24], (DEPTH, D, N_EXPERTS), D ** -0.5),
        'moe_be': nrm(ks[25], (DEPTH, N_EXPERTS), 0.01),
        'moe_w1': nrm(ks[26], (DEPTH, N_EXPERTS, D, D_EXPERT), D ** -0.5),
        'moe_w3': nrm(ks[27], (DEPTH, N_EXPERTS, D, D_EXPERT), D ** -0.5),
        'moe_w2': nrm(ks[28], (DEPTH, N_EXPERTS, D_EXPERT, D), D_EXPERT ** -0.5),
        'final_norm_g': 1.0 + nrm(ks[29], (D,), 0.02),
    }


def reference(x_prompt, x_sample, c, state_ret, state_gdn, cache_k, cache_v, c_ctx, w_mod, b_mod, norm_g, even_w_in, even_w_out, ret_decay_logit, ret_gn_w, gdn_conv_w, gdn_a_log, gdn_dt_bias, gdn_norm_w, na_w_in, na_w_out, na_rpb, moe_wg, moe_bg, moe_we, moe_be, moe_w1, moe_w3, moe_w2, final_norm_g):
    f32 = jnp.float32
    dt = x_prompt.dtype
    bp = x_prompt.shape[0]
    xp, xs = x_prompt, x_sample
    new_ret, new_gdn, new_k, new_v = [], [], [], []
    for l in range(DEPTH):
        i = l // 2
        mp = adaln(c_ctx[None, :], w_mod[l], b_mod[l])
        ms = adaln(c, w_mod[l], b_mod[l])
        hp = rmsnorm(xp, norm_g[l, 0]) * (1 + mp[:, 1]) + mp[:, 0]
        hs = rmsnorm(xs, norm_g[l, 0]) * (1 + ms[:, 1]) + ms[:, 0]
        if l % 2 == 0:
            zr = jnp.zeros((bp, 2, H_A, DK_A, DV_A), f32)
            zg = jnp.zeros((bp, 2, H_B, DK_B, DV_B), f32)
            op, sr, sg = even_mixer(hp, even_w_in[i], even_w_out[i], ret_decay_logit[i], ret_gn_w[i], gdn_conv_w[i], gdn_a_log[i], gdn_dt_bias[i], gdn_norm_w[i], zr, zg, False)
            os_, _, _ = even_mixer(hs, even_w_in[i], even_w_out[i], ret_decay_logit[i], ret_gn_w[i], gdn_conv_w[i], gdn_a_log[i], gdn_dt_bias[i], gdn_norm_w[i], state_ret[:, i], state_gdn[:, i], True)
            new_ret.append(sr.astype(dt))
            new_gdn.append(sg.astype(dt))
        else:
            qp, kp, vp = na_project(hp, na_w_in[i])
            op = context_attention(qp, kp, vp).reshape(hp.shape[0], hp.shape[1], D_C).astype(hp.dtype) @ na_w_out[i]
            qs, ks_, vs_ = na_project(hs, na_w_in[i])
            os_ = neighbourhood_attention(qs, ks_, vs_, cache_k[:, i].astype(f32), cache_v[:, i].astype(f32), na_rpb[i])
            os_ = os_.reshape(hs.shape[0], hs.shape[1], D_C).astype(hs.dtype) @ na_w_out[i]
            new_k.append(kp.astype(dt))
            new_v.append(vp.astype(dt))
        xp = xp + mp[:, 2] * op
        xs = xs + ms[:, 2] * os_
        hp = rmsnorm(xp, norm_g[l, 1]) * (1 + mp[:, 4]) + mp[:, 3]
        hs = rmsnorm(xs, norm_g[l, 1]) * (1 + ms[:, 4]) + ms[:, 3]
        xp = xp + mp[:, 5] * hier_moe(hp, moe_wg[l], moe_bg[l], moe_we[l], moe_be[l], moe_w1[l], moe_w3[l], moe_w2[l])
        xs = xs + ms[:, 5] * hier_moe(hs, moe_wg[l], moe_bg[l], moe_we[l], moe_be[l], moe_w1[l], moe_w3[l], moe_w2[l])
    y_prompt = rmsnorm(xp, final_norm_g)
    y_sample = rmsnorm(xs, final_norm_g)
    new_state_ret = jnp.stack(new_ret, axis=1)
    new_state_gdn = jnp.stack(new_gdn, axis=1)
    new_cache_k = jnp.stack(new_k, axis=1)
    new_cache_v = jnp.stack(new_v, axis=1)
    return (y_prompt, y_sample, new_state_ret, new_state_gdn, new_cache_k, new_cache_v)
```

```python
import functools

import numpy as np
import jax
import jax.numpy as jnp
from jax import lax
from jax.experimental import pallas as pl
from jax.experimental.pallas import tpu as pltpu

F32 = jnp.float32
BF16 = jnp.bfloat16
EPS = 1e-6
NEG = -1e30
CHUNK = 128
GRID_W = 64
KH = 8
KW = 16
ROPE_BASE = 10000.0
N_GROUPS = 4
EXP_PER_GROUP = 8
N_EXPERTS = N_GROUPS * EXP_PER_GROUP
MOE_ROWS = 256
VMEM_LIMIT = 56 * 1024 * 1024


def _sds(shape, dtype):
    return jax.ShapeDtypeStruct(shape, dtype)


def _cp(sem, vmem=VMEM_LIMIT):
    return pltpu.CompilerParams(dimension_semantics=sem, vmem_limit_bytes=vmem)


def _bdot(a, b):
    return jnp.dot(a.astype(BF16), b.astype(BF16), preferred_element_type=F32)


def _bdot_nt(a, b):
    return lax.dot_general(a.astype(BF16), b.astype(BF16), (((1,), (1,)), ((), ())), preferred_element_type=F32)


def _bdot_tn(a, b):
    return lax.dot_general(a.astype(BF16), b.astype(BF16), (((0,), (0,)), ((), ())), preferred_element_type=F32)


def _split2(a):
    hi = a.astype(BF16)
    lo = (a - hi.astype(F32)).astype(BF16)
    return hi, lo


def _dot3(a, b):
    ah, al = _split2(a)
    bh, bl = _split2(b)
    d = functools.partial(jnp.dot, preferred_element_type=F32)
    return d(ah, bh) + d(ah, bl) + d(al, bh)


def _blk(idx, size):
    return jnp.right_shift(idx, int(np.log2(size)))


def _silu(x):
    return x / (1.0 + jnp.exp(-x))


def _sigmoid(x):
    return 1.0 / (1.0 + jnp.exp(-x))


def _softplus(x):
    return jnp.maximum(x, 0.0) + jnp.log(1.0 + jnp.exp(-jnp.abs(x)))


def _mod_index(tm, n_prompt_rows, rows_per_sample):
    npt = n_prompt_rows // tm
    per = rows_per_sample // tm

    def f(i):
        return jnp.where(i < npt, 0, 1 + (i - npt) // per)

    return f


def _norm_mod(x, g, m_ref, shift_row, scale_row):
    r = lax.rsqrt(jnp.mean(x * x, axis=-1, keepdims=True) + EPS)
    y = x * r * g
    return y * (1.0 + m_ref[scale_row:scale_row + 1, :]) + m_ref[shift_row:shift_row + 1, :]


def _adaln_body(c_ref, w_ref, b_ref, o_ref):
    s = _silu(c_ref[...])
    o_ref[...] = _bdot(s, w_ref[...]) + b_ref[...]


def _adaln(cvec, w_mod, b_mod):
    n_l, d, d6 = w_mod.shape
    tn = 1024
    return pl.pallas_call(
        _adaln_body,
        out_shape=_sds((n_l, 8, d6), F32),
        grid=(n_l, d6 // tn),
        in_specs=[pl.BlockSpec((8, d), lambda l, j: (0, 0)),
                  pl.BlockSpec((None, d, tn), lambda l, j: (l, 0, j)),
                  pl.BlockSpec((None, 1, tn), lambda l, j: (l, 0, j))],
        out_specs=pl.BlockSpec((None, 8, tn), lambda l, j: (l, 0, j)),
        compiler_params=_cp(("parallel", "parallel")),
        name="adaln",
    )(cvec, w_mod, b_mod.reshape(n_l, 1, d6))


def _proj_body(x_ref, g_ref, m_ref, w_ref, *rest, gates, n_h):
    if gates:
        ws_ref, gp_ref, o_ref, os_ref, h_scr = rest
    else:
        o_ref, h_scr = rest

    @pl.when(pl.program_id(1) == 0)
    def _():
        hb = _norm_mod(x_ref[...], g_ref[...], m_ref, 0, 1).astype(BF16)
        h_scr[...] = hb
        if gates:
            tm = hb.shape[0]
            ps = jnp.dot(hb, ws_ref[...], preferred_element_type=F32)
            lane = lax.broadcasted_iota(jnp.int32, ps.shape, 1)
            g = -jnp.exp(gp_ref[0:1, :]) * _softplus(ps + gp_ref[1:2, :])
            ii = lax.broadcasted_iota(jnp.int32, (tm, tm), 0)
            jj = lax.broadcasted_iota(jnp.int32, (tm, tm), 1)
            same = _blk(ii, CHUNK) == _blk(jj, CHUNK)
            pre = jnp.where(same & (jj <= ii), 1.0, 0.0).astype(BF16)
            suf = jnp.where(same & (jj >= ii), 1.0, 0.0).astype(BF16)

            def tri_dot(m, v):
                v1 = v.astype(BF16)
                r1 = v - v1.astype(F32)
                v2 = r1.astype(BF16)
                v3 = (r1 - v2.astype(F32)).astype(BF16)
                d = functools.partial(jnp.dot, preferred_element_type=F32)
                return d(m, v1) + d(m, v2) + d(m, v3)

            gc = tri_dot(pre, jnp.where(lane < n_h, g, 0.0)) + tri_dot(suf, jnp.where((lane >= n_h) & (lane < 2 * n_h), g, 0.0))
            os_ref[...] = jnp.where(lane < 2 * n_h, gc, jnp.where(lane < 4 * n_h, _sigmoid(ps), 0.0))

    o_ref[...] = jnp.dot(h_scr[...], w_ref[...], preferred_element_type=F32)


def _proj(x, g, mod_l, w, modidx, tm, tn, small=None, n_h=0):
    n, d = x.shape
    nw = w.shape[1]
    in_specs = [pl.BlockSpec((tm, d), lambda i, j: (i, 0)),
                pl.BlockSpec((1, d), lambda i, j: (0, 0)),
                pl.BlockSpec((None, 6, d), lambda i, j: (modidx(i), 0, 0)),
                pl.BlockSpec((d, tn), lambda i, j: (0, j))]
    args = [x, g.reshape(1, d), mod_l, w]
    out_shape = [_sds((n, nw), F32)]
    out_specs = [pl.BlockSpec((tm, tn), lambda i, j: (i, j))]
    if small is not None:
        ws, gp = small
        in_specs += [pl.BlockSpec((d, 128), lambda i, j: (0, 0)), pl.BlockSpec((2, 128), lambda i, j: (0, 0))]
        args += [ws, gp]
        out_shape.append(_sds((n, 128), F32))
        out_specs.append(pl.BlockSpec((tm, 128), lambda i, j: (i, 0)))
    res = pl.pallas_call(
        functools.partial(_proj_body, gates=small is not None, n_h=n_h),
        out_shape=out_shape,
        grid=(n // tm, nw // tn),
        in_specs=in_specs,
        out_specs=out_specs,
        scratch_shapes=[pltpu.VMEM((tm, d), BF16)],
        compiler_params=_cp(("parallel", "arbitrary")),
        name="proj",
    )(*args)
    return res if small is not None else res[0]


def _oproj_body(*refs, n_a, gate_row):
    a_refs = refs[:n_a]
    w_refs = refs[n_a:2 * n_a]
    x_ref, m_ref, o_ref = refs[2 * n_a:]
    y = jnp.dot(a_refs[0][...], w_refs[0][...], preferred_element_type=F32)
    for a_ref, w_ref in zip(a_refs[1:], w_refs[1:]):
        y = y + jnp.dot(a_ref[...], w_ref[...], preferred_element_type=F32)
    o_ref[...] = x_ref[...] + m_ref[gate_row:gate_row + 1, :] * y


def _oproj(a_list, w_list, x, mod_l, modidx, gate_row, tm, tn):
    n, d = x.shape
    n_a = len(a_list)
    in_specs = [pl.BlockSpec((tm, a.shape[1]), lambda i, j: (i, 0)) for a in a_list]
    in_specs += [pl.BlockSpec((w.shape[0], tn), lambda i, j: (0, j)) for w in w_list]
    in_specs += [pl.BlockSpec((tm, tn), lambda i, j: (i, j)),
                 pl.BlockSpec((None, 6, tn), lambda i, j: (modidx(i), 0, j))]
    return pl.pallas_call(
        functools.partial(_oproj_body, n_a=n_a, gate_row=gate_row),
        out_shape=_sds((n, d), F32),
        grid=(n // tm, d // tn),
        in_specs=in_specs,
        out_specs=pl.BlockSpec((tm, tn), lambda i, j: (i, j)),
        compiler_params=_cp(("parallel", "arbitrary")),
        name="oproj",
    )(*a_list, *w_list, x, mod_l)


def _ret_body(lg_ref, q_ref, k_ref, v_ref, ga_ref, gw_ref, *rest, t_len, tq, latent, dk):
    if latent:
        cos_ref, sin_ref, s0_ref, o_ref = rest
    else:
        o_ref, st_ref = rest
    h = pl.program_id(1)
    lgf = lg_ref[0, h]
    lgb = lg_ref[1, h]
    half = dk // 2

    def rope(x, r0, rows):
        if not latent:
            return x
        xs = jnp.concatenate([pltpu.roll(x[:, :half], half // 2, axis=1), pltpu.roll(x[:, half:], half // 2, axis=1)], axis=1)
        return x * cos_ref[r0:r0 + rows, :] + xs * sin_ref[r0:r0 + rows, :]

    k = rope(k_ref[...], 0, t_len)
    kb = k.astype(BF16)
    vb = v_ref[...].astype(BF16)
    if latent:
        s0f = s0_ref[0].astype(BF16)
        s0b = s0_ref[1].astype(BF16)
    for qi in range(t_len // tq):
        r0 = qi * tq
        q = rope(q_ref[r0:r0 + tq, :], r0, tq) * (dk ** -0.5)
        s = _bdot_nt(q, kb)
        ii = lax.broadcasted_iota(jnp.int32, (tq, t_len), 0) + r0
        jj = lax.broadcasted_iota(jnp.int32, (tq, t_len), 1)
        dd = (ii - jj).astype(F32)
        dec = jnp.where(dd >= 0, jnp.exp(lgf * jnp.maximum(dd, 0.0)), 0.0) + jnp.where(dd <= 0, jnp.exp(lgb * jnp.maximum(-dd, 0.0)), 0.0)
        o = _bdot(s * dec, vb)
        if latent:
            pos = (lax.broadcasted_iota(jnp.int32, (tq, 1), 0) + r0).astype(F32)
            o = o + _bdot(q * jnp.exp(lgf * (pos + 1.0)), s0f) + _bdot(q * jnp.exp(lgb * (t_len - pos)), s0b)
        o = o - jnp.mean(o, axis=-1, keepdims=True)
        o = o * lax.rsqrt(jnp.mean(o * o, axis=-1, keepdims=True) + EPS)
        o_ref[r0:r0 + tq, :] = (_silu(ga_ref[r0:r0 + tq, :]) * (o * gw_ref[...])).astype(o_ref.dtype)
    if not latent:
        pos = lax.broadcasted_iota(jnp.int32, (t_len, 1), 0).astype(F32)
        st_ref[0] = _bdot_tn(k * jnp.exp(lgf * (t_len - 1.0 - pos)), vb)
        st_ref[1] = _bdot_tn(k * jnp.exp(lgb * pos), vb)


def _retention(p, log_gamma, gn_w, row0, n_seq, t_len, latent, h_a, dk, cos=None, sin=None, s0=None):
    rb = row0 // t_len
    in_specs = [pl.BlockSpec(memory_space=pltpu.SMEM),
                pl.BlockSpec((t_len, dk), lambda b, h: (rb + b, h)),
                pl.BlockSpec((t_len, dk), lambda b, h: (rb + b, h_a + h)),
                pl.BlockSpec((t_len, dk), lambda b, h: (rb + b, 2 * h_a + h)),
                pl.BlockSpec((t_len, dk), lambda b, h: (rb + b, 3 * h_a + h)),
                pl.BlockSpec((1, dk), lambda b, h: (0, h))]
    args = [log_gamma, p, p, p, p, gn_w.reshape(1, h_a * dk)]
    out_shape = [_sds((n_seq * t_len, h_a * dk), BF16)]
    out_specs = [pl.BlockSpec((t_len, dk), lambda b, h: (b, h))]
    if latent:
        in_specs += [pl.BlockSpec((t_len, dk), lambda b, h: (0, 0)),
                     pl.BlockSpec((t_len, dk), lambda b, h: (0, 0)),
                     pl.BlockSpec((None, 2, None, dk, dk), lambda b, h: (b, 0, h, 0, 0))]
        args += [cos, sin, s0]
    else:
        out_shape.append(_sds((n_seq, 2, h_a, dk, dk), F32))
        out_specs.append(pl.BlockSpec((None, 2, None, dk, dk), lambda b, h: (b, 0, h, 0, 0)))
    res = pl.pallas_call(
        functools.partial(_ret_body, t_len=t_len, tq=min(t_len, 256), latent=latent, dk=dk),
        out_shape=out_shape,
        grid=(n_seq, h_a),
        in_specs=in_specs,
        out_specs=out_specs,
        compiler_params=_cp(("parallel", "parallel")),
        name="retention_latent" if latent else "retention_ctx",
    )(*args)
    return res


def _rope_tables(t_len, dk):
    nf = dk // 4
    pos = jnp.arange(t_len)
    inv = ROPE_BASE ** (-jnp.arange(nf, dtype=F32) / nf)
    ang_r = (pos // GRID_W).astype(F32)[:, None] * inv
    ang_c = (pos % GRID_W).astype(F32)[:, None] * inv
    cr, sr, cc, sc = jnp.cos(ang_r), jnp.sin(ang_r), jnp.cos(ang_c), jnp.sin(ang_c)
    cos = jnp.concatenate([cr, cr, cc, cc], axis=1).astype(F32)
    sin = jnp.concatenate([-sr, sr, -sc, sc], axis=1).astype(F32)
    return cos, sin


def _tri_solve(lm, rhs, ii, jj):
    eye = jnp.where(ii == jj, 1.0, 0.0)
    d0 = jnp.where(_blk(ii, 16) == _blk(jj, 16), lm, 0.0)
    t = eye - d0
    p = _dot3(d0, d0)
    t = t + _dot3(t, p)
    p = _dot3(p, p)
    t = t + _dot3(t, p)
    p = _dot3(p, p)
    t = t + _dot3(t, p)
    for s in (16, 32, 64):
        e = jnp.where((_blk(ii, 2 * s) == _blk(jj, 2 * s)) & (_blk(ii, s) != _blk(jj, s)), lm, 0.0)
        t = t - _dot3(_dot3(t, e), t)
    return _dot3(t, rhs)


def _gdn_chunk(qc, kc, vc, kk, qk, gcol, bcol, state, rev, ii, jj):
    c = qc.shape[0]
    gb = jnp.broadcast_to(gcol, (c, c))
    diff = gb - gb.T
    incl = (ii <= jj) if rev else (ii >= jj)
    strict = (ii < jj) if rev else (ii > jj)
    decay = jnp.where(incl, jnp.exp(jnp.where(incl, diff, 0.0)), 0.0)
    lm = jnp.where(strict, kk * decay, 0.0) * bcol
    egc = jnp.exp(gcol)
    gtot = gcol[0:1, :] if rev else gcol[c - 1:c, :]
    rhs = jnp.concatenate([vc * bcol, kc * (bcol * egc)], axis=1)
    sol = _tri_solve(lm, rhs, ii, jj)
    dv = vc.shape[1]
    u, w = sol[:, :dv], sol[:, dv:]
    v_new = u - _bdot(w, state)
    o = _bdot(qc * egc, state) + _bdot(qk * decay, v_new)
    state = state * jnp.exp(gtot) + _bdot_tn(kc * jnp.exp(gtot - gcol), v_new)
    return o, state


def _gdn_body(q_ref, k_ref, v_ref, z_ref, ps_ref, cq_ref, ck_ref, cv_ref, nw_ref, *rest, t_len, latent, n_h, dk):
    if latent:
        s0_ref, o_ref, qs, ks, vs, oacc = rest
    else:
        o_ref, st_ref, qs, ks, vs, oacc = rest
    h = pl.program_id(1)
    row = lax.broadcasted_iota(jnp.int32, (t_len, dk), 0)

    def conv_silu(x_ref, w_ref):
        x = x_ref[...]
        acc = w_ref[2:3, :] * x
        for j in (0, 1, 3, 4):
            s = 2 - j
            xs = pltpu.roll(x, s % t_len, axis=0)
            ok = (row - s >= 0) & (row - s < t_len)
            acc = acc + w_ref[j:j + 1, :] * jnp.where(ok, xs, 0.0)
        return _silu(acc)

    q = conv_silu(q_ref, cq_ref)
    qs[...] = q * lax.rsqrt(jnp.sum(q * q, axis=-1, keepdims=True) + EPS) * (dk ** -0.5)
    k = conv_silu(k_ref, ck_ref)
    ks[...] = k * lax.rsqrt(jnp.sum(k * k, axis=-1, keepdims=True) + EPS)
    vs[...] = conv_silu(v_ref, cv_ref)
    oacc[...] = jnp.zeros_like(oacc)

    n_c = t_len // CHUNK
    ii = lax.broadcasted_iota(jnp.int32, (CHUNK, CHUNK), 0)
    jj = lax.broadcasted_iota(jnp.int32, (CHUNK, CHUNK), 1)
    lane = lax.broadcasted_iota(jnp.int32, (CHUNK, 128), 1)

    def one(c, state, d):
        r0 = pl.multiple_of(c * CHUNK, CHUNK)
        qc = qs[pl.ds(r0, CHUNK), :]
        kc = ks[pl.ds(r0, CHUNK), :]
        vc = vs[pl.ds(r0, CHUNK), :]
        ps = ps_ref[pl.ds(r0, CHUNK), :]
        gcol = jnp.sum(jnp.where(lane == d * n_h + h, ps, 0.0), axis=1, keepdims=True)
        bcol = jnp.sum(jnp.where(lane == (2 + d) * n_h + h, ps, 0.0), axis=1, keepdims=True)
        kk = _bdot_nt(kc, kc)
        qk = _bdot_nt(qc, kc)
        o, state = _gdn_chunk(qc, kc, vc, kk, qk, gcol, bcol, state, d == 1, ii, jj)
        oacc[pl.ds(r0, CHUNK), :] += o
        return state

    def step(it, carry):
        sf, sb = carry
        return one(it, sf, 0), one(n_c - 1 - it, sb, 1)

    if latent:
        init = (s0_ref[0], s0_ref[1])
    else:
        init = (jnp.zeros((dk, dk), F32), jnp.zeros((dk, dk), F32))
    sf, sb = lax.fori_loop(0, n_c, step, init)
    if not latent:
        st_ref[0] = sf
        st_ref[1] = sb
    u = oacc[...]
    u = u * lax.rsqrt(jnp.mean(u * u, axis=-1, keepdims=True) + EPS) * nw_ref[...]
    o_ref[...] = (u * _silu(z_ref[...])).astype(o_ref.dtype)


def _gdn(p, ps, conv_w, norm_w, col0, row0, n_seq, t_len, latent, n_h, dk, s0=None):
    rb = row0 // t_len
    cb = col0 // dk
    in_specs = [pl.BlockSpec((t_len, dk), lambda b, h: (rb + b, cb + h)),
                pl.BlockSpec((t_len, dk), lambda b, h: (rb + b, cb + n_h + h)),
                pl.BlockSpec((t_len, dk), lambda b, h: (rb + b, cb + 2 * n_h + h)),
                pl.BlockSpec((t_len, dk), lambda b, h: (rb + b, cb + 3 * n_h + h)),
                pl.BlockSpec((t_len, 128), lambda b, h: (rb + b, 0)),
                pl.BlockSpec((conv_w.shape[0], dk), lambda b, h: (0, h)),
                pl.BlockSpec((conv_w.shape[0], dk), lambda b, h: (0, n_h + h)),
                pl.BlockSpec((conv_w.shape[0], dk), lambda b, h: (0, 2 * n_h + h)),
                pl.BlockSpec((1, dk), lambda b, h: (0, 0))]
    args = [p, p, p, p, ps, conv_w, conv_w, conv_w, norm_w.reshape(1, dk)]
    out_shape = [_sds((n_seq * t_len, n_h * dk), BF16)]
    out_specs = [pl.BlockSpec((t_len, dk), lambda b, h: (b, h))]
    if latent:
        in_specs.append(pl.BlockSpec((None, 2, None, dk, dk), lambda b, h: (b, 0, h, 0, 0)))
        args.append(s0)
    else:
        out_shape.append(_sds((n_seq, 2, n_h, dk, dk), F32))
        out_specs.append(pl.BlockSpec((None, 2, None, dk, dk), lambda b, h: (b, 0, h, 0, 0)))
    return pl.pallas_call(
        functools.partial(_gdn_body, t_len=t_len, latent=latent, n_h=n_h, dk=dk),
        out_shape=out_shape,
        grid=(n_seq, n_h),
        in_specs=in_specs,
        out_specs=out_specs,
        scratch_shapes=[pltpu.VMEM((t_len, dk), F32)] * 4,
        compiler_params=_cp(("parallel", "parallel")),
        name="gdn_latent" if latent else "gdn_ctx",
    )(*args)


def _ctx_attn_body(q_ref, k_ref, v_ref, o_ref, *, n_h, dh):
    for h in range(n_h):
        sl = slice(h * dh, (h + 1) * dh)
        s = _bdot_nt(q_ref[:, sl] * (dh ** -0.5), k_ref[:, sl])
        m = jnp.max(s, axis=-1, keepdims=True)
        e = jnp.exp(s - m)
        o = _bdot(e, v_ref[:, sl]) / jnp.sum(e, axis=-1, keepdims=True)
        o_ref[:, sl] = o.astype(o_ref.dtype)


def _ctx_attn(p, n_seq, t_len, n_h, dh):
    d_c = n_h * dh
    return pl.pallas_call(
        functools.partial(_ctx_attn_body, n_h=n_h, dh=dh),
        out_shape=_sds((n_seq * t_len, d_c), BF16),
        grid=(n_seq,),
        in_specs=[pl.BlockSpec((t_len, d_c), lambda b: (b, 0)),
                  pl.BlockSpec((t_len, d_c), lambda b: (b, 1)),
                  pl.BlockSpec((t_len, d_c), lambda b: (b, 2))],
        out_specs=pl.BlockSpec((t_len, d_c), lambda b: (b, 0)),
        compiler_params=_cp(("parallel",)),
        name="ctx_attn",
    )(p, p, p)


def _na_body(q_ref, k_ref, v_ref, kc_ref, vc_ref, bias_ref, o_ref, *, rows, dh):
    kcb = kc_ref[...].astype(BF16)
    vcb = vc_ref[...].astype(BF16)

    def row_step(r, carry):
        r0 = jnp.clip(r - KH // 2, 0, rows - KH)
        q = q_ref[pl.ds(pl.multiple_of(r * GRID_W, GRID_W), GRID_W), :] * (dh ** -0.5)
        kb = k_ref[pl.ds(pl.multiple_of(r0 * GRID_W, GRID_W), KH * GRID_W), :]
        vb = v_ref[pl.ds(pl.multiple_of(r0 * GRID_W, GRID_W), KH * GRID_W), :]
        s_win = _bdot_nt(q, kb) + bias_ref[r0 - r + KH - 1]
        s_ctx = _bdot_nt(q, kcb)
        m = jnp.maximum(jnp.max(s_win, axis=-1, keepdims=True), jnp.max(s_ctx, axis=-1, keepdims=True))
        e_win = jnp.exp(s_win - m)
        e_ctx = jnp.exp(s_ctx - m)
        den = jnp.sum(e_win, axis=-1, keepdims=True) + jnp.sum(e_ctx, axis=-1, keepdims=True)
        o = (_bdot(e_win, vb) + _bdot(e_ctx, vcb)) / den
        o_ref[pl.ds(pl.multiple_of(r * GRID_W, GRID_W), GRID_W), :] = o.astype(o_ref.dtype)
        return carry

    lax.fori_loop(0, rows, row_step, 0)


def _na_bias_table(rpb):
    q = np.arange(GRID_W)[:, None]
    kc = np.arange(GRID_W)[None, :]
    c0 = np.clip(q - KW // 2, 0, GRID_W - KW)
    valid = (kc >= c0) & (kc < c0 + KW)
    dcol = np.clip(kc - q + KW - 1, 0, 2 * KW - 2)
    tbl = jnp.where(valid[None, None], rpb.astype(F32)[:, :, dcol], NEG)
    cat = jnp.stack([tbl[:, d0:d0 + KH] for d0 in range(KH)], axis=1)
    n_h = rpb.shape[0]
    return jnp.transpose(cat, (0, 1, 3, 2, 4)).reshape(n_h, KH, GRID_W, KH * GRID_W)


def _na_attn(p, cache_k, cache_v, layer_i, bias, row0, n_seq, t_len, n_h, dh):
    rb = row0 // t_len
    past = cache_k.shape[2]
    rows = t_len // GRID_W
    return pl.pallas_call(
        functools.partial(_na_body, rows=rows, dh=dh),
        out_shape=_sds((n_seq * t_len, n_h * dh), BF16),
        grid=(n_seq, n_h),
        in_specs=[pl.BlockSpec((t_len, dh), lambda b, h: (rb + b, h)),
                  pl.BlockSpec((t_len, dh), lambda b, h: (rb + b, n_h + h)),
                  pl.BlockSpec((t_len, dh), lambda b, h: (rb + b, 2 * n_h + h)),
                  pl.BlockSpec((None, None, past, dh), lambda b, h: (b, layer_i, 0, h)),
                  pl.BlockSpec((None, None, past, dh), lambda b, h: (b, layer_i, 0, h)),
                  pl.BlockSpec((None, KH, GRID_W, KH * GRID_W), lambda b, h: (h, 0, 0, 0))],
        out_specs=pl.BlockSpec((t_len, dh), lambda b, h: (b, h)),
        compiler_params=_cp(("parallel", "parallel")),
        name="na_attn",
    )(p, p, p, cache_k.reshape(cache_k.shape[:3] + (n_h * dh,)), cache_v.reshape(cache_v.shape[:3] + (n_h * dh,)), bias)


def _router_body(x_ref, g_ref, m_ref, wh_ref, wl_ref, b_ref, h_ref, ri_ref, rw_ref, cnt_ref, base):
    @pl.when(pl.program_id(0) == 0)
    def _():
        base[...] = jnp.zeros_like(base)

    h = _norm_mod(x_ref[...], g_ref[...], m_ref, 3, 4)
    h_ref[...] = h
    hh, hl = _split2(h)
    d = functools.partial(jnp.dot, preferred_element_type=F32)
    lg = d(hh, wh_ref[...]) + d(hh, wl_ref[...]) + d(hl, wh_ref[...]) + b_ref[...]
    tm = lg.shape[0]
    lane_i = lax.broadcasted_iota(jnp.int32, lg.shape, 1)
    lane = lane_i.astype(F32)
    big = 1e9
    gl = jnp.where(lane < N_GROUPS, lg, NEG)
    gmax = jnp.max(gl, axis=-1, keepdims=True)
    grp = jnp.min(jnp.where(gl == gmax, lane, big), axis=-1, keepdims=True)
    gate_g = 1.0 / jnp.sum(jnp.where(lane < N_GROUPS, jnp.exp(gl - gmax), 0.0), axis=-1, keepdims=True)
    lo = N_GROUPS + grp * EXP_PER_GROUP
    el = jnp.where((lane >= lo) & (lane < lo + EXP_PER_GROUP), lg, NEG)
    v1 = jnp.max(el, axis=-1, keepdims=True)
    i1 = jnp.min(jnp.where(el == v1, lane, big), axis=-1, keepdims=True)
    el2 = jnp.where(lane == i1, NEG, el)
    v2 = jnp.max(el2, axis=-1, keepdims=True)
    i2 = jnp.min(jnp.where(el2 == v2, lane, big), axis=-1, keepdims=True)
    e1 = i1 - N_GROUPS
    e2 = i2 - N_GROUPS
    t = jnp.exp(v2 - v1)
    w1 = gate_g / (1.0 + t)
    w2 = gate_g * t / (1.0 + t)
    oh = jnp.where((lane == e1) | (lane == e2), 1.0, 0.0)
    ii = lax.broadcasted_iota(jnp.int32, (tm, tm), 0)
    jj = lax.broadcasted_iota(jnp.int32, (tm, tm), 1)
    cnt = d(jnp.where(jj < ii, 1.0, 0.0).astype(BF16), oh.astype(BF16)) + base[...]
    rank1 = jnp.sum(jnp.where(lane == e1, cnt, 0.0), axis=-1, keepdims=True)
    rank2 = jnp.sum(jnp.where(lane == e2, cnt, 0.0), axis=-1, keepdims=True)
    info = jnp.where(lane == 0, e1, jnp.where(lane == 1, e2, jnp.where(lane == 2, rank1, jnp.where(lane == 3, rank2, 0.0))))
    ri_ref[...] = info.astype(jnp.int32)
    rw_ref[...] = jnp.where(lane == 0, w1, jnp.where(lane == 1, w2, 0.0))
    new_base = base[...] + jnp.sum(oh, axis=0, keepdims=True)
    base[...] = new_base
    cnt_ref[...] = new_base


def _router(x, g, mod_l, wr_hi, wr_lo, br, modidx, tm):
    n, d = x.shape
    return pl.pallas_call(
        _router_body,
        out_shape=[_sds((n, d), F32), _sds((n, 128), jnp.int32), _sds((n, 128), F32), _sds((1, 128), F32)],
        grid=(n // tm,),
        in_specs=[pl.BlockSpec((tm, d), lambda i: (i, 0)),
                  pl.BlockSpec((1, d), lambda i: (0, 0)),
                  pl.BlockSpec((None, 6, d), lambda i: (modidx(i), 0, 0)),
                  pl.BlockSpec((d, 128), lambda i: (0, 0)),
                  pl.BlockSpec((d, 128), lambda i: (0, 0)),
                  pl.BlockSpec((1, 128), lambda i: (0, 0))],
        out_specs=[pl.BlockSpec((tm, d), lambda i: (i, 0)),
                   pl.BlockSpec((tm, 128), lambda i: (i, 0)),
                   pl.BlockSpec((tm, 128), lambda i: (i, 0)),
                   pl.BlockSpec((1, 128), lambda i: (0, 0))],
        scratch_shapes=[pltpu.VMEM((1, 128), F32)],
        compiler_params=_cp(("arbitrary",)),
        name="router",
    )(x, g.reshape(1, d), mod_l, wr_hi, wr_lo, br)


def _row_copy(src, dst, i, j, sem):
    return pltpu.make_async_copy(src.at[i], dst.at[j], sem)


def _dispatch_body(sp_ref, ri_ref, h_hbm, xp_in, xp_hbm, sem, *, tm):
    del xp_in
    base = pl.program_id(0) * tm

    def issue(t, c):
        for k in (0, 1):
            dst = sp_ref[ri_ref[4 * t + k]] + ri_ref[4 * t + 2 + k]
            _row_copy(h_hbm, xp_hbm, base + t, dst, sem).start()
        return c

    def drain(t, c):
        for k in (0, 1):
            _row_copy(h_hbm, xp_hbm, 0, 0, sem).wait()
        return c

    lax.fori_loop(0, tm, issue, 0)
    lax.fori_loop(0, tm, drain, 0)


def _dispatch(starts_p, ri_flat, h3, n_rows, tm):
    n = h3.shape[0]
    xp0 = jnp.zeros((n_rows,) + h3.shape[1:], h3.dtype)
    return pl.pallas_call(
        functools.partial(_dispatch_body, tm=tm),
        out_shape=_sds(xp0.shape, xp0.dtype),
        grid_spec=pltpu.PrefetchScalarGridSpec(
            num_scalar_prefetch=1,
            grid=(n // tm,),
            in_specs=[pl.BlockSpec((4 * tm,), lambda i, sp: (i,), memory_space=pltpu.SMEM),
                      pl.BlockSpec(memory_space=pl.ANY),
                      pl.BlockSpec(memory_space=pl.ANY)],
            out_specs=pl.BlockSpec(memory_space=pl.ANY),
            scratch_shapes=[pltpu.SemaphoreType.DMA(())]),
        input_output_aliases={3: 0},
        compiler_params=_cp(("arbitrary",)),
        name="moe_dispatch",
    )(starts_p, ri_flat, h3, xp0)


def _undispatch_body(sp_ref, ri_ref, yp_hbm, ys_hbm, sem, *, tm):
    base = pl.program_id(0) * tm

    def issue(t, c):
        for k in (0, 1):
            src = sp_ref[ri_ref[4 * t + k]] + ri_ref[4 * t + 2 + k]
            _row_copy(yp_hbm, ys_hbm, src, 2 * (base + t) + k, sem).start()
        return c

    def drain(t, c):
        for k in (0, 1):
            _row_copy(yp_hbm, ys_hbm, 0, 0, sem).wait()
        return c

    lax.fori_loop(0, tm, issue, 0)
    lax.fori_loop(0, tm, drain, 0)


def _undispatch(starts_p, ri_flat, yp3, n, tm):
    return pl.pallas_call(
        functools.partial(_undispatch_body, tm=tm),
        out_shape=_sds((2 * n,) + yp3.shape[1:], yp3.dtype),
        grid_spec=pltpu.PrefetchScalarGridSpec(
            num_scalar_prefetch=1,
            grid=(n // tm,),
            in_specs=[pl.BlockSpec((4 * tm,), lambda i, sp: (i,), memory_space=pltpu.SMEM),
                      pl.BlockSpec(memory_space=pl.ANY)],
            out_specs=pl.BlockSpec(memory_space=pl.ANY),
            scratch_shapes=[pltpu.SemaphoreType.DMA(())]),
        compiler_params=_cp(("arbitrary",)),
        name="moe_undispatch",
    )(starts_p, ri_flat, yp3)


def _expert_body(be_ref, nu_ref, x_ref, w1_ref, w3_ref, w2_ref, o_ref):
    @pl.when(pl.program_id(0) < nu_ref[0])
    def _():
        xb = x_ref[...].astype(BF16)
        a = jnp.dot(xb, w1_ref[...], preferred_element_type=F32)
        b = jnp.dot(xb, w3_ref[...], preferred_element_type=F32)
        o_ref[...] = jnp.dot((_silu(a) * b).astype(BF16), w2_ref[...], preferred_element_type=F32)

    @pl.when(pl.program_id(0) >= nu_ref[0])
    def _():
        o_ref[...] = jnp.zeros_like(o_ref)


def _experts(blk_e, n_used, xp, w1, w3, w2):
    n_rows, d = xp.shape
    de = w1.shape[2]
    return pl.pallas_call(
        _expert_body,
        out_shape=_sds((n_rows, d), F32),
        grid_spec=pltpu.PrefetchScalarGridSpec(
            num_scalar_prefetch=2,
            grid=(n_rows // MOE_ROWS,),
            in_specs=[pl.BlockSpec((MOE_ROWS, d), lambda b, be, nu: (b, 0)),
                      pl.BlockSpec((None, d, de), lambda b, be, nu: (be[b], 0, 0)),
                      pl.BlockSpec((None, d, de), lambda b, be, nu: (be[b], 0, 0)),
                      pl.BlockSpec((None, de, d), lambda b, be, nu: (be[b], 0, 0))],
            out_specs=pl.BlockSpec((MOE_ROWS, d), lambda b, be, nu: (b, 0))),
        compiler_params=_cp(("arbitrary",)),
        name="moe_experts",
    )(blk_e, n_used, xp, w1, w3, w2)


def _combine_body(x_ref, ys_ref, rw_ref, m_ref, o_ref):
    d = x_ref.shape[1]
    rw = rw_ref[...]
    y = ys_ref[:, :d] * rw[:, 0:1] + ys_ref[:, d:] * rw[:, 1:2]
    o_ref[...] = x_ref[...] + m_ref[5:6, :] * y


def _combine(x, ys, rw, mod_l, modidx, tm):
    n, d = x.shape
    return pl.pallas_call(
        _combine_body,
        out_shape=_sds((n, d), F32),
        grid=(n // tm,),
        in_specs=[pl.BlockSpec((tm, d), lambda i: (i, 0)),
                  pl.BlockSpec((tm, 2 * d), lambda i: (i, 0)),
                  pl.BlockSpec((tm, 128), lambda i: (i, 0)),
                  pl.BlockSpec((None, 6, d), lambda i: (modidx(i), 0, 0))],
        out_specs=pl.BlockSpec((tm, d), lambda i: (i, 0)),
        compiler_params=_cp(("parallel",)),
        name="moe_combine",
    )(x, ys, rw, mod_l)


def _moe(x, g, mod_l, wg, bg, we, be, w1, w3, w2, modidx):
    n, d = x.shape
    tm = 256
    wr = jnp.pad(jnp.concatenate([wg, we], axis=1), ((0, 0), (0, 128 - N_GROUPS - N_EXPERTS)))
    wr_hi = wr.astype(BF16)
    wr_lo = (wr - wr_hi.astype(F32)).astype(BF16)
    br = jnp.pad(jnp.concatenate([bg, be]), (0, 128 - N_GROUPS - N_EXPERTS)).reshape(1, 128)
    h, ri, rw, cnt = _router(x, g, mod_l, wr_hi, wr_lo, br, modidx, tm)
    counts = cnt[0, :N_EXPERTS].astype(jnp.int32)
    padded = (counts + MOE_ROWS - 1) // MOE_ROWS * MOE_ROWS
    ends_p = jnp.cumsum(padded)
    starts_p = (ends_p - padded).astype(jnp.int32)
    n_blocks = (2 * n) // MOE_ROWS + N_EXPERTS
    blk_e = jnp.minimum(jnp.searchsorted(ends_p, jnp.arange(n_blocks) * MOE_ROWS, side='right'), N_EXPERTS - 1).astype(jnp.int32)
    n_used = (ends_p[-1:] // MOE_ROWS).astype(jnp.int32)
    ri_flat = ri[:, :4].reshape(-1)
    sub = d // 128
    xp3 = _dispatch(starts_p, ri_flat, h.reshape(n, sub, 128), n_blocks * MOE_ROWS, tm)
    yp = _experts(blk_e, n_used, xp3.reshape(n_blocks * MOE_ROWS, d), w1.astype(BF16), w3.astype(BF16), w2.astype(BF16))
    ys3 = _undispatch(starts_p, ri_flat, yp.reshape(n_blocks * MOE_ROWS, sub, 128), n, tm)
    return _combine(x, ys3.reshape(n, 2 * d), rw, mod_l, modidx, tm)


def _final_norm_body(x_ref, g_ref, o_ref):
    x = x_ref[...]
    o_ref[...] = x * lax.rsqrt(jnp.mean(x * x, axis=-1, keepdims=True) + EPS) * g_ref[...]


def _final_norm(x, g, tm):
    n, d = x.shape
    return pl.pallas_call(
        _final_norm_body,
        out_shape=_sds((n, d), F32),
        grid=(n // tm,),
        in_specs=[pl.BlockSpec((tm, d), lambda i: (i, 0)), pl.BlockSpec((1, d), lambda i: (0, 0))],
        out_specs=pl.BlockSpec((tm, d), lambda i: (i, 0)),
        compiler_params=_cp(("parallel",)),
        name="final_norm",
    )(x, g.reshape(1, d))


def kernel(x_prompt, x_sample, c, state_ret, state_gdn, cache_k, cache_v, c_ctx, w_mod, b_mod, norm_g, even_w_in, even_w_out, ret_decay_logit, ret_gn_w, gdn_conv_w, gdn_a_log, gdn_dt_bias, gdn_norm_w, na_w_in, na_w_out, na_rpb, moe_wg, moe_bg, moe_we, moe_be, moe_w1, moe_w3, moe_w2, final_norm_g):
    bp, seq, d = x_prompt.shape
    bs, dseq, _ = x_sample.shape
    depth = w_mod.shape[0]
    h_a, dk_a = state_ret.shape[3], state_ret.shape[4]
    h_b, dk_b = state_gdn.shape[3], state_gdn.shape[4]
    h_c, dh_c = cache_k.shape[3], cache_k.shape[4]
    a_qk = h_a * dk_a
    b_qk = h_b * dk_b
    n_main = 4 * a_qk + 4 * b_qk
    np_rows = bp * seq
    n = np_rows + bs * dseq
    assert bs + 1 <= 8 and dseq % GRID_W == 0 and dseq // GRID_W >= KH and 4 * h_b <= 128
    assert np_rows % dseq == 0 and dseq % seq == 0

    tm = 512
    modidx = _mod_index(tm, np_rows, dseq)
    modidx256 = _mod_index(256, np_rows, dseq)

    cvec = jnp.concatenate([c_ctx[None, :], c, jnp.zeros((8 - 1 - bs, d), F32)], axis=0)
    mod = _adaln(cvec, w_mod, b_mod).reshape(depth, 8, 6, d)
    x = jnp.concatenate([x_prompt.reshape(np_rows, d), x_sample.reshape(bs * dseq, d)], axis=0)
    cos, sin = _rope_tables(dseq, dk_a)

    new_ret, new_gdn, new_k, new_v = [], [], [], []
    for l in range(depth):
        i = l // 2
        mod_l = mod[l]
        if l % 2 == 0:
            w_in = even_w_in[i]
            w_main = w_in[:, :n_main].astype(BF16)
            w_small = jnp.pad(w_in[:, n_main:], ((0, 0), (0, 128 - 4 * h_b))).astype(BF16)
            gp = jnp.stack([jnp.pad(gdn_a_log[i].reshape(-1), (0, 128 - 2 * h_b)),
                            jnp.pad(gdn_dt_bias[i].reshape(-1), (0, 128 - 2 * h_b))]).astype(F32)
            p, ps = _proj(x, norm_g[l, 0], mod_l, w_main, modidx, tm, 1024, small=(w_small, gp), n_h=h_b)
            log_gamma = jax.nn.log_sigmoid(ret_decay_logit[i].astype(F32))
            a_p, sr = _retention(p, log_gamma, ret_gn_w[i], 0, bp, seq, False, h_a, dk_a)
            (a_s,) = _retention(p, log_gamma, ret_gn_w[i], np_rows, bs, dseq, True, h_a, dk_a, cos, sin, state_ret[:, i])
            b_p, sg = _gdn(p, ps, gdn_conv_w[i], gdn_norm_w[i], 4 * a_qk, 0, bp, seq, False, h_b, dk_b)
            (b_s,) = _gdn(p, ps, gdn_conv_w[i], gdn_norm_w[i], 4 * a_qk, np_rows, bs, dseq, True, h_b, dk_b, state_gdn[:, i])
            new_ret.append(sr)
            new_gdn.append(sg)
            mix_a = jnp.concatenate([a_p, a_s], axis=0)
            mix_b = jnp.concatenate([b_p, b_s], axis=0)
            w_out = even_w_out[i].astype(BF16)
            x = _oproj([mix_a, mix_b], [w_out[:a_qk], w_out[a_qk:]], x, mod_l, modidx, 2, tm, 1024)
        else:
            d_c = h_c * dh_c
            p = _proj(x, norm_g[l, 0], mod_l, na_w_in[i].astype(BF16), modidx, tm, 1024)
            o_p = _ctx_attn(p, bp, seq, h_c, dh_c)
            bias = _na_bias_table(na_rpb[i])
            o_s = _na_attn(p, cache_k, cache_v, i, bias, np_rows, bs, dseq, h_c, dh_c)
            new_k.append(p[:np_rows, d_c:2 * d_c].reshape(bp, seq, h_c, dh_c))
            new_v.append(p[:np_rows, 2 * d_c:].reshape(bp, seq, h_c, dh_c))
            x = _oproj([jnp.concatenate([o_p, o_s], axis=0)], [na_w_out[i].astype(BF16)], x, mod_l, modidx, 2, tm, 1024)
        x = _moe(x, norm_g[l, 1], mod_l, moe_wg[l], moe_bg[l], moe_we[l], moe_be[l], moe_w1[l], moe_w3[l], moe_w2[l], modidx256)

    y = _final_norm(x, final_norm_g, tm)
    return (y[:np_rows].reshape(bp, seq, d), y[np_rows:].reshape(bs, dseq, d),
            jnp.stack(new_ret, axis=1), jnp.stack(new_gdn, axis=1), jnp.stack(new_k, axis=1), jnp.stack(new_v, axis=1))
```

```python
import functools

import numpy as np
import jax
import jax.numpy as jnp
from jax import lax
from jax.experimental import pallas as pl
from jax.experimental.pallas import tpu as pltpu

F32 = jnp.float32
BF16 = jnp.bfloat16
EPS = 1e-6
NEG = -1e30
CHUNK = 128
GRID_W = 64
KH = 8
KW = 16
ROPE_BASE = 10000.0
N_GROUPS = 4
EXP_PER_GROUP = 8
N_EXPERTS = N_GROUPS * EXP_PER_GROUP
MOE_ROWS = 256
VMEM_LIMIT = 56 * 1024 * 1024


def _sds(shape, dtype):
    return jax.ShapeDtypeStruct(shape, dtype)


def _cp(sem, vmem=VMEM_LIMIT):
    return pltpu.CompilerParams(dimension_semantics=sem, vmem_limit_bytes=vmem)


def _bdot(a, b):
    return jnp.dot(a.astype(BF16), b.astype(BF16), preferred_element_type=F32)


def _bdot_nt(a, b):
    return lax.dot_general(a.astype(BF16), b.astype(BF16), (((1,), (1,)), ((), ())), preferred_element_type=F32)


def _bdot_tn(a, b):
    return lax.dot_general(a.astype(BF16), b.astype(BF16), (((0,), (0,)), ((), ())), preferred_element_type=F32)


def _split2(a):
    hi = a.astype(BF16)
    lo = (a - hi.astype(F32)).astype(BF16)
    return hi, lo


def _blk(idx, size):
    return jnp.right_shift(idx, int(np.log2(size)))


def _silu(x):
    return x / (1.0 + jnp.exp(-x))


def _sigmoid(x):
    return 1.0 / (1.0 + jnp.exp(-x))


def _softplus(x):
    return jnp.maximum(x, 0.0) + jnp.log(1.0 + jnp.exp(-jnp.abs(x)))


def _mod_index(tm, n_prompt_rows, rows_per_sample):
    npt = n_prompt_rows // tm
    per = rows_per_sample // tm

    def f(i):
        return jnp.where(i < npt, 0, 1 + (i - npt) // per)

    return f


def _norm_mod(x, g, m_ref, shift_row, scale_row):
    r = lax.rsqrt(jnp.mean(x * x, axis=-1, keepdims=True) + EPS)
    y = x * r * g
    return y * (1.0 + m_ref[scale_row:scale_row + 1, :]) + m_ref[shift_row:shift_row + 1, :]


def _adaln_body(c_ref, w_ref, b_ref, o_ref):
    s = _silu(c_ref[...])
    o_ref[...] = _bdot(s, w_ref[...]) + b_ref[...]


def _adaln(cvec, w_mod, b_mod):
    n_l, d, d6 = w_mod.shape
    tn = 1024
    return pl.pallas_call(
        _adaln_body,
        out_shape=_sds((n_l, 8, d6), F32),
        grid=(n_l, d6 // tn),
        in_specs=[pl.BlockSpec((8, d), lambda l, j: (0, 0)),
                  pl.BlockSpec((None, d, tn), lambda l, j: (l, 0, j)),
                  pl.BlockSpec((None, 1, tn), lambda l, j: (l, 0, j))],
        out_specs=pl.BlockSpec((None, 8, tn), lambda l, j: (l, 0, j)),
        compiler_params=_cp(("parallel", "parallel")),
        name="adaln",
    )(cvec, w_mod, b_mod.reshape(n_l, 1, d6))


def _proj_body(x_ref, g_ref, m_ref, w_ref, *rest, gates, n_h):
    if gates:
        ws_ref, gp_ref, o_ref, os_ref, h_scr = rest
    else:
        o_ref, h_scr = rest

    @pl.when(pl.program_id(1) == 0)
    def _():
        hb = _norm_mod(x_ref[...], g_ref[...], m_ref, 0, 1).astype(BF16)
        h_scr[...] = hb
        if gates:
            tm = hb.shape[0]
            ps = jnp.dot(hb, ws_ref[...], preferred_element_type=F32)
            lane = lax.broadcasted_iota(jnp.int32, ps.shape, 1)
            g = -jnp.exp(gp_ref[0:1, :]) * _softplus(ps + gp_ref[1:2, :])
            ii = lax.broadcasted_iota(jnp.int32, (tm, tm), 0)
            jj = lax.broadcasted_iota(jnp.int32, (tm, tm), 1)
            same = _blk(ii, CHUNK) == _blk(jj, CHUNK)
            pre = jnp.where(same & (jj <= ii), 1.0, 0.0).astype(BF16)
            suf = jnp.where(same & (jj >= ii), 1.0, 0.0).astype(BF16)

            def tri_dot(m, v):
                v1 = v.astype(BF16)
                r1 = v - v1.astype(F32)
                v2 = r1.astype(BF16)
                v3 = (r1 - v2.astype(F32)).astype(BF16)
                d = functools.partial(jnp.dot, preferred_element_type=F32)
                return d(m, v1) + d(m, v2) + d(m, v3)

            gc = tri_dot(pre, jnp.where(lane < n_h, g, 0.0)) + tri_dot(suf, jnp.where((lane >= n_h) & (lane < 2 * n_h), g, 0.0))
            os_ref[...] = jnp.where(lane < 2 * n_h, gc, jnp.where(lane < 4 * n_h, _sigmoid(ps), 0.0))

    o_ref[...] = jnp.dot(h_scr[...], w_ref[...], preferred_element_type=F32)


def _proj(x, g, mod_l, w, modidx, tm, tn, small=None, n_h=0):
    n, d = x.shape
    nw = w.shape[1]
    in_specs = [pl.BlockSpec((tm, d), lambda i, j: (i, 0)),
                pl.BlockSpec((1, d), lambda i, j: (0, 0)),
                pl.BlockSpec((None, 6, d), lambda i, j: (modidx(i), 0, 0)),
                pl.BlockSpec((d, tn), lambda i, j: (0, j))]
    args = [x, g.reshape(1, d), mod_l, w]
    out_shape = [_sds((n, nw), F32)]
    out_specs = [pl.BlockSpec((tm, tn), lambda i, j: (i, j))]
    if small is not None:
        ws, gp = small
        in_specs += [pl.BlockSpec((d, 128), lambda i, j: (0, 0)), pl.BlockSpec((2, 128), lambda i, j: (0, 0))]
        args += [ws, gp]
        out_shape.append(_sds((n, 128), F32))
        out_specs.append(pl.BlockSpec((tm, 128), lambda i, j: (i, 0)))
    res = pl.pallas_call(
        functools.partial(_proj_body, gates=small is not None, n_h=n_h),
        out_shape=out_shape,
        grid=(n // tm, nw // tn),
        in_specs=in_specs,
        out_specs=out_specs,
        scratch_shapes=[pltpu.VMEM((tm, d), BF16)],
        compiler_params=_cp(("parallel", "arbitrary")),
        name="proj",
    )(*args)
    return res if small is not None else res[0]


def _oproj_body(*refs, n_a, gate_row):
    a_refs = refs[:n_a]
    w_refs = refs[n_a:2 * n_a]
    x_ref, m_ref, o_ref = refs[2 * n_a:]
    y = jnp.dot(a_refs[0][...], w_refs[0][...], preferred_element_type=F32)
    for a_ref, w_ref in zip(a_refs[1:], w_refs[1:]):
        y = y + jnp.dot(a_ref[...], w_ref[...], preferred_element_type=F32)
    o_ref[...] = x_ref[...] + m_ref[gate_row:gate_row + 1, :] * y


def _oproj(a_list, w_list, x, mod_l, modidx, gate_row, tm, tn):
    n, d = x.shape
    n_a = len(a_list)
    in_specs = [pl.BlockSpec((tm, a.shape[1]), lambda i, j: (i, 0)) for a in a_list]
    in_specs += [pl.BlockSpec((w.shape[0], tn), lambda i, j: (0, j)) for w in w_list]
    in_specs += [pl.BlockSpec((tm, tn), lambda i, j: (i, j)),
                 pl.BlockSpec((None, 6, tn), lambda i, j: (modidx(i), 0, j))]
    return pl.pallas_call(
        functools.partial(_oproj_body, n_a=n_a, gate_row=gate_row),
        out_shape=_sds((n, d), F32),
        grid=(n // tm, d // tn),
        in_specs=in_specs,
        out_specs=pl.BlockSpec((tm, tn), lambda i, j: (i, j)),
        compiler_params=_cp(("parallel", "arbitrary")),
        name="oproj",
    )(*a_list, *w_list, x, mod_l)


def _ret_body(lg_ref, q_ref, k_ref, v_ref, ga_ref, gw_ref, *rest, t_len, tq, latent, dk):
    if latent:
        cos_ref, sin_ref, s0_ref, o_ref = rest
    else:
        o_ref, st_ref = rest
    h = pl.program_id(1)
    lgf = lg_ref[0, h]
    lgb = lg_ref[1, h]
    half = dk // 2

    def rope(x, r0, rows):
        if not latent:
            return x
        xs = jnp.concatenate([pltpu.roll(x[:, :half], half // 2, axis=1), pltpu.roll(x[:, half:], half // 2, axis=1)], axis=1)
        return x * cos_ref[r0:r0 + rows, :] + xs * sin_ref[r0:r0 + rows, :]

    k = rope(k_ref[...], 0, t_len)
    kb = k.astype(BF16)
    vb = v_ref[...].astype(BF16)
    if latent:
        s0f = s0_ref[0].astype(BF16)
        s0b = s0_ref[1].astype(BF16)
    for qi in range(t_len // tq):
        r0 = qi * tq
        q = rope(q_ref[r0:r0 + tq, :], r0, tq) * (dk ** -0.5)
        s = _bdot_nt(q, kb)
        ii = lax.broadcasted_iota(jnp.int32, (tq, t_len), 0) + r0
        jj = lax.broadcasted_iota(jnp.int32, (tq, t_len), 1)
        dd = (ii - jj).astype(F32)
        dec = jnp.where(dd >= 0, jnp.exp(lgf * jnp.maximum(dd, 0.0)), 0.0) + jnp.where(dd <= 0, jnp.exp(lgb * jnp.maximum(-dd, 0.0)), 0.0)
        o = _bdot(s * dec, vb)
        if latent:
            pos = (lax.broadcasted_iota(jnp.int32, (tq, 1), 0) + r0).astype(F32)
            o = o + _bdot(q * jnp.exp(lgf * (pos + 1.0)), s0f) + _bdot(q * jnp.exp(lgb * (t_len - pos)), s0b)
        o = o - jnp.mean(o, axis=-1, keepdims=True)
        o = o * lax.rsqrt(jnp.mean(o * o, axis=-1, keepdims=True) + EPS)
        o_ref[r0:r0 + tq, :] = (_silu(ga_ref[r0:r0 + tq, :]) * (o * gw_ref[...])).astype(o_ref.dtype)
    if not latent:
        pos = lax.broadcasted_iota(jnp.int32, (t_len, 1), 0).astype(F32)
        st_ref[0] = _bdot_tn(k * jnp.exp(lgf * (t_len - 1.0 - pos)), vb)
        st_ref[1] = _bdot_tn(k * jnp.exp(lgb * pos), vb)


def _retention(p, log_gamma, gn_w, row0, n_seq, t_len, latent, h_a, dk, cos=None, sin=None, s0=None):
    rb = row0 // t_len
    in_specs = [pl.BlockSpec(memory_space=pltpu.SMEM),
                pl.BlockSpec((t_len, dk), lambda b, h: (rb + b, h)),
                pl.BlockSpec((t_len, dk), lambda b, h: (rb + b, h_a + h)),
                pl.BlockSpec((t_len, dk), lambda b, h: (rb + b, 2 * h_a + h)),
                pl.BlockSpec((t_len, dk), lambda b, h: (rb + b, 3 * h_a + h)),
                pl.BlockSpec((1, dk), lambda b, h: (0, h))]
    args = [log_gamma, p, p, p, p, gn_w.reshape(1, h_a * dk)]
    out_shape = [_sds((n_seq * t_len, h_a * dk), BF16)]
    out_specs = [pl.BlockSpec((t_len, dk), lambda b, h: (b, h))]
    if latent:
        in_specs += [pl.BlockSpec((t_len, dk), lambda b, h: (0, 0)),
                     pl.BlockSpec((t_len, dk), lambda b, h: (0, 0)),
                     pl.BlockSpec((None, 2, None, dk, dk), lambda b, h: (b, 0, h, 0, 0))]
        args += [cos, sin, s0]
    else:
        out_shape.append(_sds((n_seq, 2, h_a, dk, dk), F32))
        out_specs.append(pl.BlockSpec((None, 2, None, dk, dk), lambda b, h: (b, 0, h, 0, 0)))
    res = pl.pallas_call(
        functools.partial(_ret_body, t_len=t_len, tq=min(t_len, 256), latent=latent, dk=dk),
        out_shape=out_shape,
        grid=(n_seq, h_a),
        in_specs=in_specs,
        out_specs=out_specs,
        compiler_params=_cp(("parallel", "parallel")),
        name="retention_latent" if latent else "retention_ctx",
    )(*args)
    return res


def _rope_tables(t_len, dk):
    nf = dk // 4
    pos = jnp.arange(t_len)
    inv = ROPE_BASE ** (-jnp.arange(nf, dtype=F32) / nf)
    ang_r = (pos // GRID_W).astype(F32)[:, None] * inv
    ang_c = (pos % GRID_W).astype(F32)[:, None] * inv
    cr, sr, cc, sc = jnp.cos(ang_r), jnp.sin(ang_r), jnp.cos(ang_c), jnp.sin(ang_c)
    cos = jnp.concatenate([cr, cr, cc, cc], axis=1).astype(F32)
    sin = jnp.concatenate([-sr, sr, -sc, sc], axis=1).astype(F32)
    return cos, sin


def _tri_solve_many(lms, rhss, ii, jj):
    eye = jnp.where(ii == jj, 1.0, 0.0)
    diag16 = _blk(ii, 16) == _blk(jj, 16)
    d0 = [jnp.where(diag16, lm, 0.0) for lm in lms]
    t = [eye - d for d in d0]
    p = [_bdot(d, d) for d in d0]
    for stage in range(3):
        t = [ti + _bdot(ti, pi) for ti, pi in zip(t, p)]
        if stage < 2:
            p = [_bdot(pi, pi) for pi in p]
    for s in (16, 32, 64):
        off = (_blk(ii, 2 * s) == _blk(jj, 2 * s)) & (_blk(ii, s) != _blk(jj, s))
        te = [_bdot(ti, jnp.where(off, lm, 0.0)) for ti, lm in zip(t, lms)]
        t = [ti - _bdot(tei, ti) for ti, tei in zip(t, te)]
    return [_bdot(ti, ri) for ti, ri in zip(t, rhss)]


def _gdn_chunk_masks(kk, gcol, bcol, rev, ii, jj):
    c = kk.shape[0]
    gb = jnp.broadcast_to(gcol, (c, c))
    diff = gb - gb.T
    incl = (ii <= jj) if rev else (ii >= jj)
    strict = (ii < jj) if rev else (ii > jj)
    decay = jnp.where(incl, jnp.exp(jnp.where(incl, diff, 0.0)), 0.0)
    return decay, jnp.where(strict, kk * decay, 0.0) * bcol


def _gdn_body(q_ref, k_ref, v_ref, z_ref, ps_ref, cq_ref, ck_ref, cv_ref, nw_ref, *rest, t_len, latent, n_h, dk, hb, cpg):
    if latent:
        s0_ref, o_ref, qs, ks, vs, oacc, u_s, w_s, a_s, qg_s, kt_s, gl_s = rest
    else:
        o_ref, st_ref, qs, ks, vs, oacc, u_s, w_s, a_s, qg_s, kt_s, gl_s = rest
    h0 = pl.program_id(1) * hb
    row = lax.broadcasted_iota(jnp.int32, (t_len, hb * dk), 0)

    def conv_silu(x_ref, w_ref):
        x = x_ref[...]
        acc = w_ref[2:3, :] * x
        for j in (0, 1, 3, 4):
            s = 2 - j
            xs = pltpu.roll(x, s % t_len, axis=0)
            ok = (row - s >= 0) & (row - s < t_len)
            acc = acc + w_ref[j:j + 1, :] * jnp.where(ok, xs, 0.0)
        return _silu(acc)

    def l2norm_heads(x, scale):
        parts = []
        for hl in range(hb):
            xh = x[:, hl * dk:(hl + 1) * dk]
            parts.append(xh * lax.rsqrt(jnp.sum(xh * xh, axis=-1, keepdims=True) + EPS) * scale)
        return parts[0] if hb == 1 else jnp.concatenate(parts, axis=1)

    qs[...] = l2norm_heads(conv_silu(q_ref, cq_ref), dk ** -0.5)
    ks[...] = l2norm_heads(conv_silu(k_ref, ck_ref), 1.0)
    vs[...] = conv_silu(v_ref, cv_ref)
    oacc[...] = jnp.zeros_like(oacc)

    n_c = t_len // CHUNK
    ii = lax.broadcasted_iota(jnp.int32, (CHUNK, CHUNK), 0)
    jj = lax.broadcasted_iota(jnp.int32, (CHUNK, CHUNK), 1)
    lane = lax.broadcasted_iota(jnp.int32, (CHUNK, 128), 1)

    def rows_of(c):
        return pl.ds(c * CHUNK if isinstance(c, int) else pl.multiple_of(c * CHUNK, CHUNK), CHUNK)

    def prepare_group(c0):
        pairs = [(hl, c0 + j) for hl in range(hb) for j in range(cpg)]
        qc = [qs[rows_of(c), hl * dk:(hl + 1) * dk] for hl, c in pairs]
        kc = [ks[rows_of(c), hl * dk:(hl + 1) * dk] for hl, c in pairs]
        vc = [vs[rows_of(c), hl * dk:(hl + 1) * dk] for hl, c in pairs]
        kk = [_bdot_nt(k, k) for k in kc]
        qk = [_bdot_nt(q, k) for q, k in zip(qc, kc)]
        probs, decays, lms, rhss, gcols = [], [], [], [], []
        for i, (hl, c) in enumerate(pairs):
            ps = ps_ref[rows_of(c), :]
            for d in (0, 1):
                gcol = jnp.sum(jnp.where(lane == d * n_h + h0 + hl, ps, 0.0), axis=1, keepdims=True)
                bcol = jnp.sum(jnp.where(lane == (2 + d) * n_h + h0 + hl, ps, 0.0), axis=1, keepdims=True)
                decay, lm = _gdn_chunk_masks(kk[i], gcol, bcol, d == 1, ii, jj)
                probs.append((i, hl, c, d))
                decays.append(decay)
                lms.append(lm)
                gcols.append(gcol)
                rhss.append(jnp.concatenate([vc[i] * bcol, kc[i] * (bcol * jnp.exp(gcol))], axis=1))
        sols = _tri_solve_many(lms, rhss, ii, jj)
        for (i, hl, c, d), decay, gcol, sol in zip(probs, decays, gcols, sols):
            gtot = gcol[0:1, :] if d == 1 else gcol[CHUNK - 1:CHUNK, :]
            idx = (hl * 2 + d) * n_c + c
            u_s[idx] = sol[:, :dk]
            w_s[idx] = sol[:, dk:].astype(BF16)
            a_s[idx] = (qk[i] * decay).astype(BF16)
            qg_s[idx] = (qc[i] * jnp.exp(gcol)).astype(BF16)
            kt_s[idx] = (kc[i] * jnp.exp(gtot - gcol)).T.astype(BF16)
            gl_s[idx] = jnp.broadcast_to(jnp.exp(gtot), (8, 128))

    if n_c == cpg:
        prepare_group(0)
    else:
        def group_step(it, carry):
            prepare_group(it * cpg)
            return carry

        lax.fori_loop(0, n_c // cpg, group_step, 0)

    chains = [(hl, d) for hl in range(hb) for d in (0, 1)]

    def step(it, states):
        idx = [(hl * 2 + d) * n_c + (n_c - 1 - it if d == 1 else it) for hl, d in chains]
        sb16 = [s.astype(BF16) for s in states]
        dot = functools.partial(jnp.dot, preferred_element_type=F32)
        v_new = [(u_s[i] - dot(w_s[i], s)).astype(BF16) for i, s in zip(idx, sb16)]
        o_st = [dot(qg_s[i], s) for i, s in zip(idx, sb16)]
        o_in = [dot(a_s[i], v) for i, v in zip(idx, v_new)]
        upd = [dot(kt_s[i], v) for i, v in zip(idx, v_new)]
        for (hl, d), a, b in zip(chains, o_st, o_in):
            c = n_c - 1 - it if d == 1 else it
            oacc[rows_of(c), hl * dk:(hl + 1) * dk] += a + b
        return tuple(s * gl_s[i][0:1, :] + x for s, i, x in zip(states, idx, upd))

    if latent:
        init = tuple(s0_ref[d, hl] for hl, d in chains)
    else:
        init = tuple(jnp.zeros((dk, dk), F32) for _ in chains)
    fin = lax.fori_loop(0, n_c, step, init)
    if not latent:
        for (hl, d), s in zip(chains, fin):
            st_ref[d, hl] = s
    u = oacc[...]
    z = z_ref[...]
    for hl in range(hb):
        sl = slice(hl * dk, (hl + 1) * dk)
        uh = u[:, sl]
        uh = uh * lax.rsqrt(jnp.mean(uh * uh, axis=-1, keepdims=True) + EPS) * nw_ref[...]
        o_ref[:, sl] = (uh * _silu(z[:, sl])).astype(o_ref.dtype)


def _gdn(p, ps, conv_w, norm_w, col0, row0, n_seq, t_len, latent, n_h, dk, s0=None):
    n_c = t_len // CHUNK
    hb, cpg = (1, 4) if n_c >= 4 else (4 // n_c, n_c)
    assert n_h % hb == 0 and n_c % cpg == 0
    wb = hb * dk
    rb = row0 // t_len
    cb = col0 // wb
    ng = n_h // hb
    in_specs = [pl.BlockSpec((t_len, wb), lambda b, h: (rb + b, cb + h)),
                pl.BlockSpec((t_len, wb), lambda b, h: (rb + b, cb + ng + h)),
                pl.BlockSpec((t_len, wb), lambda b, h: (rb + b, cb + 2 * ng + h)),
                pl.BlockSpec((t_len, wb), lambda b, h: (rb + b, cb + 3 * ng + h)),
                pl.BlockSpec((t_len, 128), lambda b, h: (rb + b, 0)),
                pl.BlockSpec((conv_w.shape[0], wb), lambda b, h: (0, h)),
                pl.BlockSpec((conv_w.shape[0], wb), lambda b, h: (0, ng + h)),
                pl.BlockSpec((conv_w.shape[0], wb), lambda b, h: (0, 2 * ng + h)),
                pl.BlockSpec((1, dk), lambda b, h: (0, 0))]
    args = [p, p, p, p, ps, conv_w, conv_w, conv_w, norm_w.reshape(1, dk)]
    out_shape = [_sds((n_seq * t_len, n_h * dk), BF16)]
    out_specs = [pl.BlockSpec((t_len, wb), lambda b, h: (b, h))]
    if latent:
        in_specs.append(pl.BlockSpec((None, 2, hb, dk, dk), lambda b, h: (b, 0, h, 0, 0)))
        args.append(s0)
    else:
        out_shape.append(_sds((n_seq, 2, n_h, dk, dk), F32))
        out_specs.append(pl.BlockSpec((None, 2, hb, dk, dk), lambda b, h: (b, 0, h, 0, 0)))
    n_p = 2 * hb * n_c
    return pl.pallas_call(
        functools.partial(_gdn_body, t_len=t_len, latent=latent, n_h=n_h, dk=dk, hb=hb, cpg=cpg),
        out_shape=out_shape,
        grid=(n_seq, ng),
        in_specs=in_specs,
        out_specs=out_specs,
        scratch_shapes=[pltpu.VMEM((t_len, wb), F32)] * 4
        + [pltpu.VMEM((n_p, CHUNK, dk), F32), pltpu.VMEM((n_p, CHUNK, dk), BF16),
           pltpu.VMEM((n_p, CHUNK, CHUNK), BF16), pltpu.VMEM((n_p, CHUNK, dk), BF16),
           pltpu.VMEM((n_p, dk, CHUNK), BF16), pltpu.VMEM((n_p, 8, 128), F32)],
        compiler_params=_cp(("parallel", "parallel")),
        name="gdn_latent" if latent else "gdn_ctx",
    )(*args)


def _ctx_attn_body(q_ref, k_ref, v_ref, o_ref, *, n_h, dh):
    for h in range(n_h):
        sl = slice(h * dh, (h + 1) * dh)
        s = _bdot_nt(q_ref[:, sl] * (dh ** -0.5), k_ref[:, sl])
        m = jnp.max(s, axis=-1, keepdims=True)
        e = jnp.exp(s - m)
        o = _bdot(e, v_ref[:, sl]) / jnp.sum(e, axis=-1, keepdims=True)
        o_ref[:, sl] = o.astype(o_ref.dtype)


def _ctx_attn(p, n_seq, t_len, n_h, dh):
    d_c = n_h * dh
    return pl.pallas_call(
        functools.partial(_ctx_attn_body, n_h=n_h, dh=dh),
        out_shape=_sds((n_seq * t_len, d_c), BF16),
        grid=(n_seq,),
        in_specs=[pl.BlockSpec((t_len, d_c), lambda b: (b, 0)),
                  pl.BlockSpec((t_len, d_c), lambda b: (b, 1)),
                  pl.BlockSpec((t_len, d_c), lambda b: (b, 2))],
        out_specs=pl.BlockSpec((t_len, d_c), lambda b: (b, 0)),
        compiler_params=_cp(("parallel",)),
        name="ctx_attn",
    )(p, p, p)


def _na_body(q_ref, k_ref, v_ref, kc_ref, vc_ref, bias_ref, o_ref, *, rows, dh):
    kcb = kc_ref[...].astype(BF16)
    vcb = vc_ref[...].astype(BF16)

    rg = 4
    gw = rg * GRID_W

    def group_step(it, carry):
        q_all = q_ref[pl.ds(pl.multiple_of(it * gw, gw), gw), :] * (dh ** -0.5)
        s_ctx = _bdot_nt(q_all, kcb)
        r0s, s_wins = [], []
        for j in range(rg):
            r = it * rg + j
            r0 = jnp.clip(r - KH // 2, 0, rows - KH)
            kb = k_ref[pl.ds(pl.multiple_of(r0 * GRID_W, GRID_W), KH * GRID_W), :]
            s_wins.append(_bdot_nt(q_all[j * GRID_W:(j + 1) * GRID_W], kb) + bias_ref[r0 - r + KH - 1])
            r0s.append(r0)
        s_win = jnp.concatenate(s_wins, axis=0)
        m = jnp.maximum(jnp.max(s_win, axis=-1, keepdims=True), jnp.max(s_ctx, axis=-1, keepdims=True))
        e_win = jnp.exp(s_win - m)
        e_ctx = jnp.exp(s_ctx - m)
        den = jnp.sum(e_win, axis=-1, keepdims=True) + jnp.sum(e_ctx, axis=-1, keepdims=True)
        e_win = e_win.astype(BF16)
        o_ctx = _bdot(e_ctx, vcb)
        o_wins = []
        for j in range(rg):
            vb = v_ref[pl.ds(pl.multiple_of(r0s[j] * GRID_W, GRID_W), KH * GRID_W), :]
            o_wins.append(_bdot(e_win[j * GRID_W:(j + 1) * GRID_W], vb))
        o = (jnp.concatenate(o_wins, axis=0) + o_ctx) / den
        o_ref[pl.ds(pl.multiple_of(it * gw, gw), gw), :] = o.astype(o_ref.dtype)
        return carry

    lax.fori_loop(0, rows // rg, group_step, 0)


def _na_bias_table(rpb):
    q = np.arange(GRID_W)[:, None]
    kc = np.arange(GRID_W)[None, :]
    c0 = np.clip(q - KW // 2, 0, GRID_W - KW)
    valid = (kc >= c0) & (kc < c0 + KW)
    dcol = np.clip(kc - q + KW - 1, 0, 2 * KW - 2)
    tbl = jnp.where(valid[None, None], rpb.astype(F32)[:, :, dcol], NEG)
    cat = jnp.stack([tbl[:, d0:d0 + KH] for d0 in range(KH)], axis=1)
    n_h = rpb.shape[0]
    return jnp.transpose(cat, (0, 1, 3, 2, 4)).reshape(n_h, KH, GRID_W, KH * GRID_W)


def _na_attn(p, cache_k, cache_v, layer_i, bias, row0, n_seq, t_len, n_h, dh):
    rb = row0 // t_len
    past = cache_k.shape[2]
    rows = t_len // GRID_W
    return pl.pallas_call(
        functools.partial(_na_body, rows=rows, dh=dh),
        out_shape=_sds((n_seq * t_len, n_h * dh), BF16),
        grid=(n_seq, n_h),
        in_specs=[pl.BlockSpec((t_len, dh), lambda b, h: (rb + b, h)),
                  pl.BlockSpec((t_len, dh), lambda b, h: (rb + b, n_h + h)),
                  pl.BlockSpec((t_len, dh), lambda b, h: (rb + b, 2 * n_h + h)),
                  pl.BlockSpec((None, None, past, dh), lambda b, h: (b, layer_i, 0, h)),
                  pl.BlockSpec((None, None, past, dh), lambda b, h: (b, layer_i, 0, h)),
                  pl.BlockSpec((None, KH, GRID_W, KH * GRID_W), lambda b, h: (h, 0, 0, 0))],
        out_specs=pl.BlockSpec((t_len, dh), lambda b, h: (b, h)),
        compiler_params=_cp(("parallel", "parallel")),
        name="na_attn",
    )(p, p, p, cache_k.reshape(cache_k.shape[:3] + (n_h * dh,)), cache_v.reshape(cache_v.shape[:3] + (n_h * dh,)), bias)


def _router_body(x_ref, g_ref, m_ref, wh_ref, wl_ref, b_ref, ri_ref, rw_ref, cnt_ref, base):
    @pl.when(pl.program_id(0) == 0)
    def _():
        base[...] = jnp.zeros_like(base)

    h = _norm_mod(x_ref[...], g_ref[...], m_ref, 3, 4)
    hh, hl = _split2(h)
    d = functools.partial(jnp.dot, preferred_element_type=F32)
    lg = d(hh, wh_ref[...]) + d(hh, wl_ref[...]) + d(hl, wh_ref[...]) + b_ref[...]
    tm = lg.shape[0]
    lane_i = lax.broadcasted_iota(jnp.int32, lg.shape, 1)
    lane = lane_i.astype(F32)
    big = 1e9
    gl = jnp.where(lane < N_GROUPS, lg, NEG)
    gmax = jnp.max(gl, axis=-1, keepdims=True)
    grp = jnp.min(jnp.where(gl == gmax, lane, big), axis=-1, keepdims=True)
    gate_g = 1.0 / jnp.sum(jnp.where(lane < N_GROUPS, jnp.exp(gl - gmax), 0.0), axis=-1, keepdims=True)
    lo = N_GROUPS + grp * EXP_PER_GROUP
    el = jnp.where((lane >= lo) & (lane < lo + EXP_PER_GROUP), lg, NEG)
    v1 = jnp.max(el, axis=-1, keepdims=True)
    i1 = jnp.min(jnp.where(el == v1, lane, big), axis=-1, keepdims=True)
    el2 = jnp.where(lane == i1, NEG, el)
    v2 = jnp.max(el2, axis=-1, keepdims=True)
    i2 = jnp.min(jnp.where(el2 == v2, lane, big), axis=-1, keepdims=True)
    e1 = i1 - N_GROUPS
    e2 = i2 - N_GROUPS
    t = jnp.exp(v2 - v1)
    w1 = gate_g / (1.0 + t)
    w2 = gate_g * t / (1.0 + t)
    oh = jnp.where((lane == e1) | (lane == e2), 1.0, 0.0)
    ii = lax.broadcasted_iota(jnp.int32, (tm, tm), 0)
    jj = lax.broadcasted_iota(jnp.int32, (tm, tm), 1)
    cnt = d(jnp.where(jj < ii, 1.0, 0.0).astype(BF16), oh.astype(BF16)) + base[...]
    rank1 = jnp.sum(jnp.where(lane == e1, cnt, 0.0), axis=-1, keepdims=True)
    rank2 = jnp.sum(jnp.where(lane == e2, cnt, 0.0), axis=-1, keepdims=True)
    info = jnp.where(lane == 0, e1, jnp.where(lane == 1, e2, jnp.where(lane == 2, rank1, jnp.where(lane == 3, rank2, 0.0))))
    ri_ref[...] = info.astype(jnp.int32)
    rw_ref[...] = jnp.where(lane == 0, w1, jnp.where(lane == 1, w2, 0.0))
    new_base = base[...] + jnp.sum(oh, axis=0, keepdims=True)
    base[...] = new_base
    cnt_ref[...] = new_base


def _router(x, g, mod_l, wr_hi, wr_lo, br, modidx, tm):
    n, d = x.shape
    return pl.pallas_call(
        _router_body,
        out_shape=[_sds((n, 128), jnp.int32), _sds((n, 128), F32), _sds((1, 128), F32)],
        grid=(n // tm,),
        in_specs=[pl.BlockSpec((tm, d), lambda i: (i, 0)),
                  pl.BlockSpec((1, d), lambda i: (0, 0)),
                  pl.BlockSpec((None, 6, d), lambda i: (modidx(i), 0, 0)),
                  pl.BlockSpec((d, 128), lambda i: (0, 0)),
                  pl.BlockSpec((d, 128), lambda i: (0, 0)),
                  pl.BlockSpec((1, 128), lambda i: (0, 0))],
        out_specs=[pl.BlockSpec((tm, 128), lambda i: (i, 0)),
                   pl.BlockSpec((tm, 128), lambda i: (i, 0)),
                   pl.BlockSpec((1, 128), lambda i: (0, 0))],
        scratch_shapes=[pltpu.VMEM((1, 128), F32)],
        compiler_params=_cp(("arbitrary",)),
        name="router",
    )(x, g.reshape(1, d), mod_l, wr_hi, wr_lo, br)


def _slab_rows(row, sub):
    start = row * sub
    return pl.ds(start if isinstance(start, int) else pl.multiple_of(start, sub), sub)


def _slab_copy(src, src_row, dst, dst_row, sub, sem):
    return pltpu.make_async_copy(src.at[_slab_rows(src_row, sub)], dst.at[_slab_rows(dst_row, sub)], sem)


def _slot_row(sp_ref, ri_ref, t, k):
    return sp_ref[ri_ref[4 * t + k]] + ri_ref[4 * t + 2 + k]


def _dispatch_body(sp_ref, ri_ref, x_ref, g_ref, m_ref, xp_in, xp_hbm, slab, sem, *, tm, sub):
    del xp_in
    h = _norm_mod(x_ref[...], g_ref[...], m_ref, 3, 4)
    for c in range(sub):
        slab[pl.ds(c, tm, stride=sub), :] = h[:, c * 128:(c + 1) * 128]

    def issue(t, carry):
        for k in (0, 1):
            _slab_copy(slab, t, xp_hbm, _slot_row(sp_ref, ri_ref, t, k), sub, sem).start()
        return carry

    def drain(t, carry):
        for k in (0, 1):
            _slab_copy(slab, 0, xp_hbm, 0, sub, sem).wait()
        return carry

    lax.fori_loop(0, tm, issue, 0)
    lax.fori_loop(0, tm, drain, 0)


def _dispatch(starts_p, ri_flat, x, g, mod_l, modidx, n_rows, tm):
    n, d = x.shape
    sub = d // 128
    xp0 = jnp.zeros((n_rows * sub, 128), F32)
    return pl.pallas_call(
        functools.partial(_dispatch_body, tm=tm, sub=sub),
        out_shape=_sds(xp0.shape, xp0.dtype),
        grid_spec=pltpu.PrefetchScalarGridSpec(
            num_scalar_prefetch=1,
            grid=(n // tm,),
            in_specs=[pl.BlockSpec((4 * tm,), lambda i, sp: (i,), memory_space=pltpu.SMEM),
                      pl.BlockSpec((tm, d), lambda i, sp: (i, 0)),
                      pl.BlockSpec((1, d), lambda i, sp: (0, 0)),
                      pl.BlockSpec((None, 6, d), lambda i, sp: (modidx(i), 0, 0)),
                      pl.BlockSpec(memory_space=pl.ANY)],
            out_specs=pl.BlockSpec(memory_space=pl.ANY),
            scratch_shapes=[pltpu.VMEM((tm * sub, 128), F32), pltpu.SemaphoreType.DMA(())]),
        input_output_aliases={5: 0},
        compiler_params=_cp(("arbitrary",)),
        name="moe_dispatch",
    )(starts_p, ri_flat, x, g.reshape(1, d), mod_l, xp0)


def _expert_body(be_ref, nu_ref, x_ref, w1_ref, w3_ref, w2_ref, o_ref, *, rows, sub):
    @pl.when(pl.program_id(0) < nu_ref[0])
    def _():
        xb = jnp.concatenate([x_ref[pl.ds(c, rows, stride=sub), :] for c in range(sub)], axis=1).astype(BF16)
        a = jnp.dot(xb, w1_ref[...], preferred_element_type=F32)
        b = jnp.dot(xb, w3_ref[...], preferred_element_type=F32)
        y = jnp.dot((_silu(a) * b).astype(BF16), w2_ref[...], preferred_element_type=F32)
        for c in range(sub):
            o_ref[pl.ds(c, rows, stride=sub), :] = y[:, c * 128:(c + 1) * 128]

    @pl.when(pl.program_id(0) >= nu_ref[0])
    def _():
        o_ref[...] = jnp.zeros_like(o_ref)


def _experts(blk_e, n_used, xp, w1, w3, w2):
    d, de = w1.shape[1], w1.shape[2]
    sub = d // 128
    n_rows = xp.shape[0] // sub
    return pl.pallas_call(
        functools.partial(_expert_body, rows=MOE_ROWS, sub=sub),
        out_shape=_sds(xp.shape, F32),
        grid_spec=pltpu.PrefetchScalarGridSpec(
            num_scalar_prefetch=2,
            grid=(n_rows // MOE_ROWS,),
            in_specs=[pl.BlockSpec((MOE_ROWS * sub, 128), lambda b, be, nu: (b, 0)),
                      pl.BlockSpec((None, d, de), lambda b, be, nu: (be[b], 0, 0)),
                      pl.BlockSpec((None, d, de), lambda b, be, nu: (be[b], 0, 0)),
                      pl.BlockSpec((None, de, d), lambda b, be, nu: (be[b], 0, 0))],
            out_specs=pl.BlockSpec((MOE_ROWS * sub, 128), lambda b, be, nu: (b, 0))),
        compiler_params=_cp(("arbitrary",)),
        name="moe_experts",
    )(blk_e, n_used, xp, w1, w3, w2)


def _combine_body(sp_ref, ri_ref, x_ref, rw_ref, m_ref, yp_hbm, o_ref, slab, sem, *, tm, sub):
    def issue(t, carry):
        for k in (0, 1):
            _slab_copy(yp_hbm, _slot_row(sp_ref, ri_ref, t, k), slab, 2 * t + k, sub, sem).start()
        return carry

    def drain(t, carry):
        for k in (0, 1):
            _slab_copy(yp_hbm, 0, slab, 0, sub, sem).wait()
        return carry

    lax.fori_loop(0, tm, issue, 0)
    lax.fori_loop(0, tm, drain, 0)
    rw = rw_ref[...]
    w1, w2 = rw[:, 0:1], rw[:, 1:2]
    for c in range(sub):
        sl = slice(c * 128, (c + 1) * 128)
        y = slab[pl.ds(c, tm, stride=2 * sub), :] * w1 + slab[pl.ds(sub + c, tm, stride=2 * sub), :] * w2
        o_ref[:, sl] = x_ref[:, sl] + m_ref[5:6, sl] * y


def _combine(starts_p, ri_flat, x, rw, mod_l, yp, modidx, tm):
    n, d = x.shape
    sub = d // 128
    return pl.pallas_call(
        functools.partial(_combine_body, tm=tm, sub=sub),
        out_shape=_sds((n, d), F32),
        grid_spec=pltpu.PrefetchScalarGridSpec(
            num_scalar_prefetch=1,
            grid=(n // tm,),
            in_specs=[pl.BlockSpec((4 * tm,), lambda i, sp: (i,), memory_space=pltpu.SMEM),
                      pl.BlockSpec((tm, d), lambda i, sp: (i, 0)),
                      pl.BlockSpec((tm, 128), lambda i, sp: (i, 0)),
                      pl.BlockSpec((None, 6, d), lambda i, sp: (modidx(i), 0, 0)),
                      pl.BlockSpec(memory_space=pl.ANY)],
            out_specs=pl.BlockSpec((tm, d), lambda i, sp: (i, 0)),
            scratch_shapes=[pltpu.VMEM((2 * tm * sub, 128), F32), pltpu.SemaphoreType.DMA(())]),
        compiler_params=_cp(("arbitrary",)),
        name="moe_combine",
    )(starts_p, ri_flat, x, rw, mod_l, yp)


def _moe(x, g, mod_l, wg, bg, we, be, w1, w3, w2, modidx):
    n, d = x.shape
    tm = 256
    wr = jnp.pad(jnp.concatenate([wg, we], axis=1), ((0, 0), (0, 128 - N_GROUPS - N_EXPERTS)))
    wr_hi = wr.astype(BF16)
    wr_lo = (wr - wr_hi.astype(F32)).astype(BF16)
    br = jnp.pad(jnp.concatenate([bg, be]), (0, 128 - N_GROUPS - N_EXPERTS)).reshape(1, 128)
    ri, rw, cnt = _router(x, g, mod_l, wr_hi, wr_lo, br, modidx, tm)
    counts = cnt[0, :N_EXPERTS].astype(jnp.int32)
    padded = (counts + MOE_ROWS - 1) // MOE_ROWS * MOE_ROWS
    ends_p = jnp.cumsum(padded)
    starts_p = (ends_p - padded).astype(jnp.int32)
    n_blocks = (2 * n) // MOE_ROWS + N_EXPERTS
    blk_start = jnp.arange(n_blocks, dtype=jnp.int32) * MOE_ROWS
    blk_e = jnp.minimum(jnp.sum((ends_p[None, :] <= blk_start[:, None]).astype(jnp.int32), axis=1), N_EXPERTS - 1)
    n_used = (ends_p[-1:] // MOE_ROWS).astype(jnp.int32)
    ri_flat = ri[:, :4].reshape(-1)
    xp = _dispatch(starts_p, ri_flat, x, g, mod_l, modidx, n_blocks * MOE_ROWS, tm)
    yp = _experts(blk_e, n_used, xp, w1.astype(BF16), w3.astype(BF16), w2.astype(BF16))
    return _combine(starts_p, ri_flat, x, rw, mod_l, yp, modidx, tm)


def _final_norm_body(x_ref, g_ref, o_ref):
    x = x_ref[...]
    o_ref[...] = x * lax.rsqrt(jnp.mean(x * x, axis=-1, keepdims=True) + EPS) * g_ref[...]


def _final_norm(x, g, tm):
    n, d = x.shape
    return pl.pallas_call(
        _final_norm_body,
        out_shape=_sds((n, d), F32),
        grid=(n // tm,),
        in_specs=[pl.BlockSpec((tm, d), lambda i: (i, 0)), pl.BlockSpec((1, d), lambda i: (0, 0))],
        out_specs=pl.BlockSpec((tm, d), lambda i: (i, 0)),
        compiler_params=_cp(("parallel",)),
        name="final_norm",
    )(x, g.reshape(1, d))


def kernel(x_prompt, x_sample, c, state_ret, state_gdn, cache_k, cache_v, c_ctx, w_mod, b_mod, norm_g, even_w_in, even_w_out, ret_decay_logit, ret_gn_w, gdn_conv_w, gdn_a_log, gdn_dt_bias, gdn_norm_w, na_w_in, na_w_out, na_rpb, moe_wg, moe_bg, moe_we, moe_be, moe_w1, moe_w3, moe_w2, final_norm_g):
    bp, seq, d = x_prompt.shape
    bs, dseq, _ = x_sample.shape
    depth = w_mod.shape[0]
    h_a, dk_a = state_ret.shape[3], state_ret.shape[4]
    h_b, dk_b = state_gdn.shape[3], state_gdn.shape[4]
    h_c, dh_c = cache_k.shape[3], cache_k.shape[4]
    a_qk = h_a * dk_a
    b_qk = h_b * dk_b
    n_main = 4 * a_qk + 4 * b_qk
    np_rows = bp * seq
    n = np_rows + bs * dseq
    assert bs + 1 <= 8 and dseq % GRID_W == 0 and dseq // GRID_W >= KH and 4 * h_b <= 128
    assert np_rows % dseq == 0 and dseq % seq == 0

    tm = 512
    modidx = _mod_index(tm, np_rows, dseq)
    modidx256 = _mod_index(256, np_rows, dseq)

    cvec = jnp.concatenate([c_ctx[None, :], c, jnp.zeros((8 - 1 - bs, d), F32)], axis=0)
    mod = _adaln(cvec, w_mod, b_mod).reshape(depth, 8, 6, d)
    x = jnp.concatenate([x_prompt.reshape(np_rows, d), x_sample.reshape(bs * dseq, d)], axis=0)
    cos, sin = _rope_tables(dseq, dk_a)

    new_ret, new_gdn, new_k, new_v = [], [], [], []
    for l in range(depth):
        i = l // 2
        mod_l = mod[l]
        if l % 2 == 0:
            w_in = even_w_in[i]
            w_main = w_in[:, :n_main].astype(BF16)
            w_small = jnp.pad(w_in[:, n_main:], ((0, 0), (0, 128 - 4 * h_b))).astype(BF16)
            gp = jnp.stack([jnp.pad(gdn_a_log[i].reshape(-1), (0, 128 - 2 * h_b)),
                            jnp.pad(gdn_dt_bias[i].reshape(-1), (0, 128 - 2 * h_b))]).astype(F32)
            p, ps = _proj(x, norm_g[l, 0], mod_l, w_main, modidx, tm, 1024, small=(w_small, gp), n_h=h_b)
            log_gamma = jax.nn.log_sigmoid(ret_decay_logit[i].astype(F32))
            a_p, sr = _retention(p, log_gamma, ret_gn_w[i], 0, bp, seq, False, h_a, dk_a)
            (a_s,) = _retention(p, log_gamma, ret_gn_w[i], np_rows, bs, dseq, True, h_a, dk_a, cos, sin, state_ret[:, i])
            b_p, sg = _gdn(p, ps, gdn_conv_w[i], gdn_norm_w[i], 4 * a_qk, 0, bp, seq, False, h_b, dk_b)
            (b_s,) = _gdn(p, ps, gdn_conv_w[i], gdn_norm_w[i], 4 * a_qk, np_rows, bs, dseq, True, h_b, dk_b, state_gdn[:, i])
            new_ret.append(sr)
            new_gdn.append(sg)
            mix_a = jnp.concatenate([a_p, a_s], axis=0)
            mix_b = jnp.concatenate([b_p, b_s], axis=0)
            w_out = even_w_out[i].astype(BF16)
            x = _oproj([mix_a, mix_b], [w_out[:a_qk], w_out[a_qk:]], x, mod_l, modidx, 2, tm, 1024)
        else:
            d_c = h_c * dh_c
            p = _proj(x, norm_g[l, 0], mod_l, na_w_in[i].astype(BF16), modidx, tm, 1024)
            o_p = _ctx_attn(p, bp, seq, h_c, dh_c)
            bias = _na_bias_table(na_rpb[i])
            o_s = _na_attn(p, cache_k, cache_v, i, bias, np_rows, bs, dseq, h_c, dh_c)
            new_k.append(p[:np_rows, d_c:2 * d_c].reshape(bp, seq, h_c, dh_c))
            new_v.append(p[:np_rows, 2 * d_c:].reshape(bp, seq, h_c, dh_c))
            x = _oproj([jnp.concatenate([o_p, o_s], axis=0)], [na_w_out[i].astype(BF16)], x, mod_l, modidx, 2, tm, 1024)
        x = _moe(x, norm_g[l, 1], mod_l, moe_wg[l], moe_bg[l], moe_we[l], moe_be[l], moe_w1[l], moe_w3[l], moe_w2[l], modidx256)

    y = _final_norm(x, final_norm_g, tm)
    return (y[:np_rows].reshape(bp, seq, d), y[np_rows:].reshape(bs, dseq, d),
            jnp.stack(new_ret, axis=1), jnp.stack(new_gdn, axis=1), jnp.stack(new_k, axis=1), jnp.stack(new_v, axis=1))
```

```python
import functools

import numpy as np
import jax
import jax.numpy as jnp
from jax import lax
from jax.experimental import pallas as pl
from jax.experimental.pallas import tpu as pltpu

F32 = jnp.float32
BF16 = jnp.bfloat16
EPS = 1e-6
NEG = -1e30
CHUNK = 128
GRID_W = 64
KH = 8
KW = 16
ROPE_BASE = 10000.0
N_GROUPS = 4
EXP_PER_GROUP = 8
N_EXPERTS = N_GROUPS * EXP_PER_GROUP
MOE_ROWS = 256
VMEM_LIMIT = 56 * 1024 * 1024


def _sds(shape, dtype):
    return jax.ShapeDtypeStruct(shape, dtype)


def _cp(sem, vmem=VMEM_LIMIT):
    return pltpu.CompilerParams(dimension_semantics=sem, vmem_limit_bytes=vmem)


def _bdot(a, b):
    return jnp.dot(a.astype(BF16), b.astype(BF16), preferred_element_type=F32)


def _bdot_nt(a, b):
    return lax.dot_general(a.astype(BF16), b.astype(BF16), (((1,), (1,)), ((), ())), preferred_element_type=F32)


def _bdot_tn(a, b):
    return lax.dot_general(a.astype(BF16), b.astype(BF16), (((0,), (0,)), ((), ())), preferred_element_type=F32)


def _split2(a):
    hi = a.astype(BF16)
    lo = (a - hi.astype(F32)).astype(BF16)
    return hi, lo


def _blk(idx, size):
    return jnp.right_shift(idx, int(np.log2(size)))


def _silu(x):
    return x / (1.0 + jnp.exp(-x))


def _sigmoid(x):
    return 1.0 / (1.0 + jnp.exp(-x))


def _softplus(x):
    return jnp.maximum(x, 0.0) + jnp.log(1.0 + jnp.exp(-jnp.abs(x)))


def _mod_index(tm, n_prompt_rows, rows_per_sample):
    npt = n_prompt_rows // tm
    per = rows_per_sample // tm

    def f(i):
        return jnp.where(i < npt, 0, 1 + (i - npt) // per)

    return f


def _norm_mod(x, g, m_ref, shift_row, scale_row):
    r = lax.rsqrt(jnp.mean(x * x, axis=-1, keepdims=True) + EPS)
    y = x * r * g
    return y * (1.0 + m_ref[scale_row:scale_row + 1, :]) + m_ref[shift_row:shift_row + 1, :]


def _adaln_body(c_ref, w_ref, b_ref, o_ref):
    s = _silu(c_ref[...])
    o_ref[...] = _bdot(s, w_ref[...]) + b_ref[...]


def _adaln(cvec, w_mod, b_mod):
    n_l, d, d6 = w_mod.shape
    tn = 1024
    return pl.pallas_call(
        _adaln_body,
        out_shape=_sds((n_l, 8, d6), F32),
        grid=(n_l, d6 // tn),
        in_specs=[pl.BlockSpec((8, d), lambda l, j: (0, 0)),
                  pl.BlockSpec((None, d, tn), lambda l, j: (l, 0, j)),
                  pl.BlockSpec((None, 1, tn), lambda l, j: (l, 0, j))],
        out_specs=pl.BlockSpec((None, 8, tn), lambda l, j: (l, 0, j)),
        compiler_params=_cp(("parallel", "parallel")),
        name="adaln",
    )(cvec, w_mod, b_mod.reshape(n_l, 1, d6))


def _proj_body(x_ref, g_ref, m_ref, w_ref, *rest, gates, n_h):
    if gates:
        ws_ref, gp_ref, o_ref, os_ref, h_scr = rest
    else:
        o_ref, h_scr = rest

    @pl.when(pl.program_id(1) == 0)
    def _():
        rows = 256
        for r0 in range(0, x_ref.shape[0], rows):
            rs = slice(r0, r0 + rows)
            hb = _norm_mod(x_ref[rs, :], g_ref[...], m_ref, 0, 1).astype(BF16)
            h_scr[rs, :] = hb
            if gates:
                os_ref[rs, :] = _gdn_gates(jnp.dot(hb, ws_ref[...], preferred_element_type=F32), gp_ref, n_h)

    o_ref[...] = jnp.dot(h_scr[...], w_ref[...], preferred_element_type=F32)


def _gdn_gates(ps, gp_ref, n_h):
    lane = lax.broadcasted_iota(jnp.int32, ps.shape, 1)
    g = -jnp.exp(gp_ref[0:1, :]) * _softplus(ps + gp_ref[1:2, :])
    ii = lax.broadcasted_iota(jnp.int32, (CHUNK, CHUNK), 0)
    jj = lax.broadcasted_iota(jnp.int32, (CHUNK, CHUNK), 1)
    pre = jnp.where(jj <= ii, 1.0, 0.0).astype(BF16)
    suf = jnp.where(jj >= ii, 1.0, 0.0).astype(BF16)

    def tri_dot(m, v):
        v1 = v.astype(BF16)
        r1 = v - v1.astype(F32)
        v2 = r1.astype(BF16)
        v3 = (r1 - v2.astype(F32)).astype(BF16)
        d = functools.partial(jnp.dot, preferred_element_type=F32)
        return d(m, v1) + d(m, v2) + d(m, v3)

    g_f = jnp.where(lane < n_h, g, 0.0)
    g_b = jnp.where((lane >= n_h) & (lane < 2 * n_h), g, 0.0)
    gc = jnp.concatenate([tri_dot(pre, g_f[c0:c0 + CHUNK]) + tri_dot(suf, g_b[c0:c0 + CHUNK])
                          for c0 in range(0, ps.shape[0], CHUNK)], axis=0)
    return jnp.where(lane < 2 * n_h, gc, jnp.where(lane < 4 * n_h, _sigmoid(ps), 0.0))


def _mod_spec(layer, modidx, width, col=None):
    def tail(i, *rest):
        return (modidx(i), 0, 0 if col is None else col(i, *rest))

    return pl.BlockSpec((None, None, 6, width), lambda *idx: (layer,) + tail(*idx))


def _proj(x, g_all, gi, mod, layer, w_all, wi, modidx, tm, tn, small=None, n_h=0):
    n, d = x.shape
    nw = w_all.shape[2]
    in_specs = [pl.BlockSpec((tm, d), lambda i, j: (i, 0)),
                pl.BlockSpec((None, 1, d), lambda i, j: (gi, 0, 0)),
                _mod_spec(layer, modidx, d),
                pl.BlockSpec((None, d, tn), lambda i, j: (wi, 0, j))]
    args = [x, g_all, mod, w_all]
    out_shape = [_sds((n, nw), F32)]
    out_specs = [pl.BlockSpec((tm, tn), lambda i, j: (i, j))]
    if small is not None:
        ws, gp, si = small
        in_specs += [pl.BlockSpec((None, d, 128), lambda i, j: (si, 0, 0)), pl.BlockSpec((None, 2, 128), lambda i, j: (si, 0, 0))]
        args += [ws, gp]
        out_shape.append(_sds((n, 128), F32))
        out_specs.append(pl.BlockSpec((tm, 128), lambda i, j: (i, 0)))
    res = pl.pallas_call(
        functools.partial(_proj_body, gates=small is not None, n_h=n_h),
        out_shape=out_shape,
        grid=(n // tm, nw // tn),
        in_specs=in_specs,
        out_specs=out_specs,
        scratch_shapes=[pltpu.VMEM((tm, d), BF16)],
        compiler_params=_cp(("parallel", "arbitrary")),
        name="proj",
    )(*args)
    return res if small is not None else res[0]


def _oproj_body(*refs, n_a, gate_row):
    a_refs = refs[:n_a]
    w_refs = refs[n_a:2 * n_a]
    x_ref, m_ref, o_ref = refs[2 * n_a:]
    y = jnp.dot(a_refs[0][...], w_refs[0][...], preferred_element_type=F32)
    for a_ref, w_ref in zip(a_refs[1:], w_refs[1:]):
        y = y + jnp.dot(a_ref[...], w_ref[...], preferred_element_type=F32)
    o_ref[...] = x_ref[...] + m_ref[gate_row:gate_row + 1, :] * y


def _oproj(a_list, w_all, wi, x, mod, layer, modidx, gate_row, tm, tn):
    n, d = x.shape
    n_a = len(a_list)
    in_specs = [pl.BlockSpec((tm, a.shape[1]), lambda i, j: (i, 0)) for a in a_list]
    k0 = 0
    for a in a_list:
        ka = a.shape[1]
        assert k0 % ka == 0
        in_specs.append(pl.BlockSpec((None, ka, tn), lambda i, j, kb=k0 // ka: (wi, kb, j)))
        k0 += ka
    in_specs += [pl.BlockSpec((tm, tn), lambda i, j: (i, j)),
                 _mod_spec(layer, modidx, tn, col=lambda i, j: j)]
    return pl.pallas_call(
        functools.partial(_oproj_body, n_a=n_a, gate_row=gate_row),
        out_shape=_sds((n, d), F32),
        grid=(n // tm, d // tn),
        in_specs=in_specs,
        out_specs=pl.BlockSpec((tm, tn), lambda i, j: (i, j)),
        compiler_params=_cp(("parallel", "arbitrary")),
        name="oproj",
    )(*a_list, *([w_all] * n_a), x, mod)


def _ret_body(lg_ref, q_ref, k_ref, v_ref, ga_ref, gw_ref, *rest, t_len, tq, latent, dk, li):
    if latent:
        cos_ref, sin_ref, s0_ref, o_ref = rest
    else:
        o_ref, st_ref = rest
    h = pl.program_id(1)
    lgf = lg_ref[2 * li, h]
    lgb = lg_ref[2 * li + 1, h]
    half = dk // 2

    def rope(x, r0, rows):
        if not latent:
            return x
        xs = jnp.concatenate([pltpu.roll(x[:, :half], half // 2, axis=1), pltpu.roll(x[:, half:], half // 2, axis=1)], axis=1)
        return x * cos_ref[r0:r0 + rows, :] + xs * sin_ref[r0:r0 + rows, :]

    k = rope(k_ref[...], 0, t_len)
    kb = k.astype(BF16)
    vb = v_ref[...].astype(BF16)
    if latent:
        s0f = s0_ref[0].astype(BF16)
        s0b = s0_ref[1].astype(BF16)
    for qi in range(t_len // tq):
        r0 = qi * tq
        q = rope(q_ref[r0:r0 + tq, :], r0, tq) * (dk ** -0.5)
        s = _bdot_nt(q, kb)
        ii = lax.broadcasted_iota(jnp.int32, (tq, t_len), 0) + r0
        jj = lax.broadcasted_iota(jnp.int32, (tq, t_len), 1)
        dd = (ii - jj).astype(F32)
        dec = jnp.where(dd >= 0, jnp.exp(lgf * jnp.maximum(dd, 0.0)), 0.0) + jnp.where(dd <= 0, jnp.exp(lgb * jnp.maximum(-dd, 0.0)), 0.0)
        o = _bdot(s * dec, vb)
        if latent:
            pos = (lax.broadcasted_iota(jnp.int32, (tq, 1), 0) + r0).astype(F32)
            o = o + _bdot(q * jnp.exp(lgf * (pos + 1.0)), s0f) + _bdot(q * jnp.exp(lgb * (t_len - pos)), s0b)
        o = o - jnp.mean(o, axis=-1, keepdims=True)
        o = o * lax.rsqrt(jnp.mean(o * o, axis=-1, keepdims=True) + EPS)
        o_ref[r0:r0 + tq, :] = (_silu(ga_ref[r0:r0 + tq, :]) * (o * gw_ref[...])).astype(o_ref.dtype)
    if not latent:
        pos = lax.broadcasted_iota(jnp.int32, (t_len, 1), 0).astype(F32)
        st_ref[0] = _bdot_tn(k * jnp.exp(lgf * (t_len - 1.0 - pos)), vb)
        st_ref[1] = _bdot_tn(k * jnp.exp(lgb * pos), vb)


def _retention(p, log_gamma, gn_w, li, row0, n_seq, t_len, latent, h_a, dk, cos=None, sin=None, s0=None):
    rb = row0 // t_len
    in_specs = [pl.BlockSpec(memory_space=pltpu.SMEM),
                pl.BlockSpec((t_len, dk), lambda b, h: (rb + b, h)),
                pl.BlockSpec((t_len, dk), lambda b, h: (rb + b, h_a + h)),
                pl.BlockSpec((t_len, dk), lambda b, h: (rb + b, 2 * h_a + h)),
                pl.BlockSpec((t_len, dk), lambda b, h: (rb + b, 3 * h_a + h)),
                pl.BlockSpec((None, 1, dk), lambda b, h: (li, 0, h))]
    args = [log_gamma, p, p, p, p, gn_w]
    out_shape = [_sds((n_seq * t_len, h_a * dk), BF16)]
    out_specs = [pl.BlockSpec((t_len, dk), lambda b, h: (b, h))]
    if latent:
        in_specs += [pl.BlockSpec((t_len, dk), lambda b, h: (0, 0)),
                     pl.BlockSpec((t_len, dk), lambda b, h: (0, 0)),
                     pl.BlockSpec((None, None, 2, None, dk, dk), lambda b, h: (b, li, 0, h, 0, 0))]
        args += [cos, sin, s0]
    else:
        out_shape.append(_sds((n_seq, 2, h_a, dk, dk), F32))
        out_specs.append(pl.BlockSpec((None, 2, None, dk, dk), lambda b, h: (b, 0, h, 0, 0)))
    res = pl.pallas_call(
        functools.partial(_ret_body, t_len=t_len, tq=min(t_len, 256), latent=latent, dk=dk, li=li),
        out_shape=out_shape,
        grid=(n_seq, h_a),
        in_specs=in_specs,
        out_specs=out_specs,
        compiler_params=_cp(("parallel", "parallel")),
        name="retention_latent" if latent else "retention_ctx",
    )(*args)
    return res


def _rope_tables(t_len, dk):
    nf = dk // 4
    pos = jnp.arange(t_len)
    inv = ROPE_BASE ** (-jnp.arange(nf, dtype=F32) / nf)
    ang_r = (pos // GRID_W).astype(F32)[:, None] * inv
    ang_c = (pos % GRID_W).astype(F32)[:, None] * inv
    cr, sr, cc, sc = jnp.cos(ang_r), jnp.sin(ang_r), jnp.cos(ang_c), jnp.sin(ang_c)
    cos = jnp.concatenate([cr, cr, cc, cc], axis=1).astype(F32)
    sin = jnp.concatenate([-sr, sr, -sc, sc], axis=1).astype(F32)
    return cos, sin


def _tri_solve_many(lms, rhss, ii, jj):
    eye = jnp.where(ii == jj, 1.0, 0.0)
    diag16 = _blk(ii, 16) == _blk(jj, 16)
    d0 = [jnp.where(diag16, lm, 0.0) for lm in lms]
    t = [eye - d for d in d0]
    p = [_bdot(d, d) for d in d0]
    for stage in range(3):
        t = [ti + _bdot(ti, pi) for ti, pi in zip(t, p)]
        if stage < 2:
            p = [_bdot(pi, pi) for pi in p]
    for s in (16, 32, 64):
        off = (_blk(ii, 2 * s) == _blk(jj, 2 * s)) & (_blk(ii, s) != _blk(jj, s))
        te = [_bdot(ti, jnp.where(off, lm, 0.0)) for ti, lm in zip(t, lms)]
        t = [ti - _bdot(tei, ti) for ti, tei in zip(t, te)]
    return [_bdot(ti, ri) for ti, ri in zip(t, rhss)]


def _gdn_chunk_masks(kk, gcol, bcol, rev, ii, jj):
    c = kk.shape[0]
    gb = jnp.broadcast_to(gcol, (c, c))
    diff = gb - gb.T
    incl = (ii <= jj) if rev else (ii >= jj)
    strict = (ii < jj) if rev else (ii > jj)
    decay = jnp.where(incl, jnp.exp(jnp.where(incl, diff, 0.0)), 0.0)
    return decay, jnp.where(strict, kk * decay, 0.0) * bcol


def _gdn_body(q_ref, k_ref, v_ref, z_ref, ps_ref, cq_ref, ck_ref, cv_ref, nw_ref, *rest, t_len, latent, n_h, dk, hb, cpg):
    if latent:
        s0_ref, o_ref, qs, ks, vs, oacc, u_s, w_s, a_s, qg_s, kt_s, gl_s = rest
    else:
        o_ref, st_ref, qs, ks, vs, oacc, u_s, w_s, a_s, qg_s, kt_s, gl_s = rest
    h0 = pl.program_id(1) * hb
    row = lax.broadcasted_iota(jnp.int32, (t_len, hb * dk), 0)

    def conv_silu(x_ref, w_ref):
        x = x_ref[...]
        acc = w_ref[2:3, :] * x
        for j in (0, 1, 3, 4):
            s = 2 - j
            xs = pltpu.roll(x, s % t_len, axis=0)
            ok = (row - s >= 0) & (row - s < t_len)
            acc = acc + w_ref[j:j + 1, :] * jnp.where(ok, xs, 0.0)
        return _silu(acc)

    def l2norm_heads(x, scale):
        parts = []
        for hl in range(hb):
            xh = x[:, hl * dk:(hl + 1) * dk]
            parts.append(xh * lax.rsqrt(jnp.sum(xh * xh, axis=-1, keepdims=True) + EPS) * scale)
        return parts[0] if hb == 1 else jnp.concatenate(parts, axis=1)

    qs[...] = l2norm_heads(conv_silu(q_ref, cq_ref), dk ** -0.5)
    ks[...] = l2norm_heads(conv_silu(k_ref, ck_ref), 1.0)
    vs[...] = conv_silu(v_ref, cv_ref)
    oacc[...] = jnp.zeros_like(oacc)

    n_c = t_len // CHUNK
    ii = lax.broadcasted_iota(jnp.int32, (CHUNK, CHUNK), 0)
    jj = lax.broadcasted_iota(jnp.int32, (CHUNK, CHUNK), 1)
    lane = lax.broadcasted_iota(jnp.int32, (CHUNK, 128), 1)

    def rows_of(c):
        return pl.ds(c * CHUNK if isinstance(c, int) else pl.multiple_of(c * CHUNK, CHUNK), CHUNK)

    def prepare_group(c0):
        pairs = [(hl, c0 + j) for hl in range(hb) for j in range(cpg)]
        qc = [qs[rows_of(c), hl * dk:(hl + 1) * dk] for hl, c in pairs]
        kc = [ks[rows_of(c), hl * dk:(hl + 1) * dk] for hl, c in pairs]
        vc = [vs[rows_of(c), hl * dk:(hl + 1) * dk] for hl, c in pairs]
        kk = [_bdot_nt(k, k) for k in kc]
        qk = [_bdot_nt(q, k) for q, k in zip(qc, kc)]
        probs, decays, lms, rhss, gcols = [], [], [], [], []
        for i, (hl, c) in enumerate(pairs):
            ps = ps_ref[rows_of(c), :]
            for d in (0, 1):
                gcol = jnp.sum(jnp.where(lane == d * n_h + h0 + hl, ps, 0.0), axis=1, keepdims=True)
                bcol = jnp.sum(jnp.where(lane == (2 + d) * n_h + h0 + hl, ps, 0.0), axis=1, keepdims=True)
                decay, lm = _gdn_chunk_masks(kk[i], gcol, bcol, d == 1, ii, jj)
                probs.append((i, hl, c, d))
                decays.append(decay)
                lms.append(lm)
                gcols.append(gcol)
                rhss.append(jnp.concatenate([vc[i] * bcol, kc[i] * (bcol * jnp.exp(gcol))], axis=1))
        sols = _tri_solve_many(lms, rhss, ii, jj)
        for (i, hl, c, d), decay, gcol, sol in zip(probs, decays, gcols, sols):
            gtot = gcol[0:1, :] if d == 1 else gcol[CHUNK - 1:CHUNK, :]
            idx = (hl * 2 + d) * n_c + c
            u_s[idx] = sol[:, :dk]
            w_s[idx] = sol[:, dk:].astype(BF16)
            a_s[idx] = (qk[i] * decay).astype(BF16)
            qg_s[idx] = (qc[i] * jnp.exp(gcol)).astype(BF16)
            kt_s[idx] = (kc[i] * jnp.exp(gtot - gcol)).T.astype(BF16)
            gl_s[idx] = jnp.broadcast_to(jnp.exp(gtot), (8, 128))

    if n_c == cpg:
        prepare_group(0)
    else:
        def group_step(it, carry):
            prepare_group(it * cpg)
            return carry

        lax.fori_loop(0, n_c // cpg, group_step, 0)

    chains = [(hl, d) for hl in range(hb) for d in (0, 1)]

    def step(it, states):
        idx = [(hl * 2 + d) * n_c + (n_c - 1 - it if d == 1 else it) for hl, d in chains]
        sb16 = [s.astype(BF16) for s in states]
        dot = functools.partial(jnp.dot, preferred_element_type=F32)
        v_new = [(u_s[i] - dot(w_s[i], s)).astype(BF16) for i, s in zip(idx, sb16)]
        o_st = [dot(qg_s[i], s) for i, s in zip(idx, sb16)]
        o_in = [dot(a_s[i], v) for i, v in zip(idx, v_new)]
        upd = [dot(kt_s[i], v) for i, v in zip(idx, v_new)]
        for (hl, d), a, b in zip(chains, o_st, o_in):
            c = n_c - 1 - it if d == 1 else it
            oacc[rows_of(c), hl * dk:(hl + 1) * dk] += a + b
        return tuple(s * gl_s[i][0:1, :] + x for s, i, x in zip(states, idx, upd))

    if latent:
        init = tuple(s0_ref[d, hl] for hl, d in chains)
    else:
        init = tuple(jnp.zeros((dk, dk), F32) for _ in chains)
    fin = lax.fori_loop(0, n_c, step, init)
    if not latent:
        for (hl, d), s in zip(chains, fin):
            st_ref[d, hl] = s
    u = oacc[...]
    z = z_ref[...]
    for hl in range(hb):
        sl = slice(hl * dk, (hl + 1) * dk)
        uh = u[:, sl]
        uh = uh * lax.rsqrt(jnp.mean(uh * uh, axis=-1, keepdims=True) + EPS) * nw_ref[...]
        o_ref[:, sl] = (uh * _silu(z[:, sl])).astype(o_ref.dtype)


def _gdn(p, ps, conv_w, norm_w, li, col0, row0, n_seq, t_len, latent, n_h, dk, s0=None):
    n_c = t_len // CHUNK
    hb, cpg = (1, 4) if n_c >= 4 else (4 // n_c, n_c)
    assert n_h % hb == 0 and n_c % cpg == 0
    wb = hb * dk
    rb = row0 // t_len
    cb = col0 // wb
    ng = n_h // hb
    in_specs = [pl.BlockSpec((t_len, wb), lambda b, h: (rb + b, cb + h)),
                pl.BlockSpec((t_len, wb), lambda b, h: (rb + b, cb + ng + h)),
                pl.BlockSpec((t_len, wb), lambda b, h: (rb + b, cb + 2 * ng + h)),
                pl.BlockSpec((t_len, wb), lambda b, h: (rb + b, cb + 3 * ng + h)),
                pl.BlockSpec((t_len, 128), lambda b, h: (rb + b, 0)),
                pl.BlockSpec((None, conv_w.shape[1], wb), lambda b, h: (li, 0, h)),
                pl.BlockSpec((None, conv_w.shape[1], wb), lambda b, h: (li, 0, ng + h)),
                pl.BlockSpec((None, conv_w.shape[1], wb), lambda b, h: (li, 0, 2 * ng + h)),
                pl.BlockSpec((None, 1, dk), lambda b, h: (li, 0, 0))]
    args = [p, p, p, p, ps, conv_w, conv_w, conv_w, norm_w]
    out_shape = [_sds((n_seq * t_len, n_h * dk), BF16)]
    out_specs = [pl.BlockSpec((t_len, wb), lambda b, h: (b, h))]
    if latent:
        in_specs.append(pl.BlockSpec((None, None, 2, hb, dk, dk), lambda b, h: (b, li, 0, h, 0, 0)))
        args.append(s0)
    else:
        out_shape.append(_sds((n_seq, 2, n_h, dk, dk), F32))
        out_specs.append(pl.BlockSpec((None, 2, hb, dk, dk), lambda b, h: (b, 0, h, 0, 0)))
    n_p = 2 * hb * n_c
    return pl.pallas_call(
        functools.partial(_gdn_body, t_len=t_len, latent=latent, n_h=n_h, dk=dk, hb=hb, cpg=cpg),
        out_shape=out_shape,
        grid=(n_seq, ng),
        in_specs=in_specs,
        out_specs=out_specs,
        scratch_shapes=[pltpu.VMEM((t_len, wb), F32)] * 4
        + [pltpu.VMEM((n_p, CHUNK, dk), F32), pltpu.VMEM((n_p, CHUNK, dk), BF16),
           pltpu.VMEM((n_p, CHUNK, CHUNK), BF16), pltpu.VMEM((n_p, CHUNK, dk), BF16),
           pltpu.VMEM((n_p, dk, CHUNK), BF16), pltpu.VMEM((n_p, 8, 128), F32)],
        compiler_params=_cp(("parallel", "parallel")),
        name="gdn_latent" if latent else "gdn_ctx",
    )(*args)


def _ctx_attn_body(q_ref, k_ref, v_ref, o_ref, *, n_h, dh):
    for h in range(n_h):
        sl = slice(h * dh, (h + 1) * dh)
        s = _bdot_nt(q_ref[:, sl] * (dh ** -0.5), k_ref[:, sl])
        m = jnp.max(s, axis=-1, keepdims=True)
        e = jnp.exp(s - m)
        o = _bdot(e, v_ref[:, sl]) / jnp.sum(e, axis=-1, keepdims=True)
        o_ref[:, sl] = o.astype(o_ref.dtype)


def _ctx_attn(p, n_seq, t_len, n_h, dh):
    d_c = n_h * dh
    return pl.pallas_call(
        functools.partial(_ctx_attn_body, n_h=n_h, dh=dh),
        out_shape=_sds((n_seq * t_len, d_c), BF16),
        grid=(n_seq,),
        in_specs=[pl.BlockSpec((t_len, d_c), lambda b: (b, 0)),
                  pl.BlockSpec((t_len, d_c), lambda b: (b, 1)),
                  pl.BlockSpec((t_len, d_c), lambda b: (b, 2))],
        out_specs=pl.BlockSpec((t_len, d_c), lambda b: (b, 0)),
        compiler_params=_cp(("parallel",)),
        name="ctx_attn",
    )(p, p, p)


def _na_body(q_ref, k_ref, v_ref, kc_ref, vc_ref, bias_ref, o_ref, *, rows, dh):
    kcb = kc_ref[...].astype(BF16)
    vcb = vc_ref[...].astype(BF16)

    rg = 4
    gw = rg * GRID_W

    def group_step(it, carry):
        q_all = q_ref[pl.ds(pl.multiple_of(it * gw, gw), gw), :] * (dh ** -0.5)
        s_ctx = _bdot_nt(q_all, kcb)
        r0s, s_wins = [], []
        for j in range(rg):
            r = it * rg + j
            r0 = jnp.clip(r - KH // 2, 0, rows - KH)
            kb = k_ref[pl.ds(pl.multiple_of(r0 * GRID_W, GRID_W), KH * GRID_W), :]
            s_wins.append(_bdot_nt(q_all[j * GRID_W:(j + 1) * GRID_W], kb) + bias_ref[r0 - r + KH - 1])
            r0s.append(r0)
        s_win = jnp.concatenate(s_wins, axis=0)
        m = jnp.maximum(jnp.max(s_win, axis=-1, keepdims=True), jnp.max(s_ctx, axis=-1, keepdims=True))
        e_win = jnp.exp(s_win - m)
        e_ctx = jnp.exp(s_ctx - m)
        den = jnp.sum(e_win, axis=-1, keepdims=True) + jnp.sum(e_ctx, axis=-1, keepdims=True)
        e_win = e_win.astype(BF16)
        o_ctx = _bdot(e_ctx, vcb)
        o_wins = []
        for j in range(rg):
            vb = v_ref[pl.ds(pl.multiple_of(r0s[j] * GRID_W, GRID_W), KH * GRID_W), :]
            o_wins.append(_bdot(e_win[j * GRID_W:(j + 1) * GRID_W], vb))
        o = (jnp.concatenate(o_wins, axis=0) + o_ctx) / den
        o_ref[pl.ds(pl.multiple_of(it * gw, gw), gw), :] = o.astype(o_ref.dtype)
        return carry

    lax.fori_loop(0, rows // rg, group_step, 0)


def _na_bias_table(rpb):
    q = np.arange(GRID_W)[:, None]
    kc = np.arange(GRID_W)[None, :]
    c0 = np.clip(q - KW // 2, 0, GRID_W - KW)
    valid = (kc >= c0) & (kc < c0 + KW)
    dcol = np.clip(kc - q + KW - 1, 0, 2 * KW - 2)
    tbl = jnp.where(valid[None, None, None], rpb.astype(F32)[:, :, :, dcol], NEG)
    cat = jnp.stack([tbl[:, :, d0:d0 + KH] for d0 in range(KH)], axis=2)
    n_l, n_h = rpb.shape[:2]
    return jnp.transpose(cat, (0, 1, 2, 4, 3, 5)).reshape(n_l, n_h, KH, GRID_W, KH * GRID_W)


def _na_attn(p, cache_k, cache_v, layer_i, bias, row0, n_seq, t_len, n_h, dh):
    rb = row0 // t_len
    past = cache_k.shape[2]
    rows = t_len // GRID_W
    return pl.pallas_call(
        functools.partial(_na_body, rows=rows, dh=dh),
        out_shape=_sds((n_seq * t_len, n_h * dh), BF16),
        grid=(n_seq, n_h),
        in_specs=[pl.BlockSpec((t_len, dh), lambda b, h: (rb + b, h)),
                  pl.BlockSpec((t_len, dh), lambda b, h: (rb + b, n_h + h)),
                  pl.BlockSpec((t_len, dh), lambda b, h: (rb + b, 2 * n_h + h)),
                  pl.BlockSpec((None, None, past, dh), lambda b, h: (b, layer_i, 0, h)),
                  pl.BlockSpec((None, None, past, dh), lambda b, h: (b, layer_i, 0, h)),
                  pl.BlockSpec((None, None, KH, GRID_W, KH * GRID_W), lambda b, h: (layer_i, h, 0, 0, 0))],
        out_specs=pl.BlockSpec((t_len, dh), lambda b, h: (b, h)),
        compiler_params=_cp(("parallel", "parallel")),
        name="na_attn",
    )(p, p, p, cache_k.reshape(cache_k.shape[:3] + (n_h * dh,)), cache_v.reshape(cache_v.shape[:3] + (n_h * dh,)), bias)


def _router_body(x_ref, g_ref, m_ref, wh_ref, wl_ref, b_ref, ri_ref, rw_ref, cnt_ref, base):
    @pl.when(pl.program_id(0) == 0)
    def _():
        base[...] = jnp.zeros_like(base)

    h = _norm_mod(x_ref[...], g_ref[...], m_ref, 3, 4)
    hh, hl = _split2(h)
    d = functools.partial(jnp.dot, preferred_element_type=F32)
    lg = d(hh, wh_ref[...]) + d(hh, wl_ref[...]) + d(hl, wh_ref[...]) + b_ref[...]
    tm = lg.shape[0]
    lane_i = lax.broadcasted_iota(jnp.int32, lg.shape, 1)
    lane = lane_i.astype(F32)
    big = 1e9
    gl = jnp.where(lane < N_GROUPS, lg, NEG)
    gmax = jnp.max(gl, axis=-1, keepdims=True)
    grp = jnp.min(jnp.where(gl == gmax, lane, big), axis=-1, keepdims=True)
    gate_g = 1.0 / jnp.sum(jnp.where(lane < N_GROUPS, jnp.exp(gl - gmax), 0.0), axis=-1, keepdims=True)
    lo = N_GROUPS + grp * EXP_PER_GROUP
    el = jnp.where((lane >= lo) & (lane < lo + EXP_PER_GROUP), lg, NEG)
    v1 = jnp.max(el, axis=-1, keepdims=True)
    i1 = jnp.min(jnp.where(el == v1, lane, big), axis=-1, keepdims=True)
    el2 = jnp.where(lane == i1, NEG, el)
    v2 = jnp.max(el2, axis=-1, keepdims=True)
    i2 = jnp.min(jnp.where(el2 == v2, lane, big), axis=-1, keepdims=True)
    e1 = i1 - N_GROUPS
    e2 = i2 - N_GROUPS
    t = jnp.exp(v2 - v1)
    w1 = gate_g / (1.0 + t)
    w2 = gate_g * t / (1.0 + t)
    oh = jnp.where((lane == e1) | (lane == e2), 1.0, 0.0)
    ii = lax.broadcasted_iota(jnp.int32, (tm, tm), 0)
    jj = lax.broadcasted_iota(jnp.int32, (tm, tm), 1)
    cnt = d(jnp.where(jj < ii, 1.0, 0.0).astype(BF16), oh.astype(BF16)) + base[...]
    rank1 = jnp.sum(jnp.where(lane == e1, cnt, 0.0), axis=-1, keepdims=True)
    rank2 = jnp.sum(jnp.where(lane == e2, cnt, 0.0), axis=-1, keepdims=True)
    info = jnp.where(lane == 0, e1, jnp.where(lane == 1, e2, jnp.where(lane == 2, rank1, jnp.where(lane == 3, rank2, 0.0))))
    ri_ref[...] = info.astype(jnp.int32)
    rw_ref[...] = jnp.where(lane == 0, w1, jnp.where(lane == 1, w2, 0.0))
    new_base = base[...] + jnp.sum(oh, axis=0, keepdims=True)
    base[...] = new_base
    cnt_ref[...] = new_base


def _router(x, g_all, gi, mod, layer, wr_hi, wr_lo, br, modidx, tm):
    n, d = x.shape
    return pl.pallas_call(
        _router_body,
        out_shape=[_sds((n, 128), jnp.int32), _sds((n, 128), F32), _sds((1, 128), F32)],
        grid=(n // tm,),
        in_specs=[pl.BlockSpec((tm, d), lambda i: (i, 0)),
                  pl.BlockSpec((None, 1, d), lambda i: (gi, 0, 0)),
                  _mod_spec(layer, modidx, d),
                  pl.BlockSpec((None, d, 128), lambda i: (layer, 0, 0)),
                  pl.BlockSpec((None, d, 128), lambda i: (layer, 0, 0)),
                  pl.BlockSpec((None, 1, 128), lambda i: (layer, 0, 0))],
        out_specs=[pl.BlockSpec((tm, 128), lambda i: (i, 0)),
                   pl.BlockSpec((tm, 128), lambda i: (i, 0)),
                   pl.BlockSpec((1, 128), lambda i: (0, 0))],
        scratch_shapes=[pltpu.VMEM((1, 128), F32)],
        compiler_params=_cp(("arbitrary",)),
        name="router",
    )(x, g_all, mod, wr_hi, wr_lo, br)


def _slab_rows(row, sub):
    start = row * sub
    return pl.ds(start if isinstance(start, int) else pl.multiple_of(start, sub), sub)


def _slab_copy(src, src_row, dst, dst_row, sub, sem):
    return pltpu.make_async_copy(src.at[_slab_rows(src_row, sub)], dst.at[_slab_rows(dst_row, sub)], sem)


def _slot_row(sp_ref, ri_ref, t, k):
    return sp_ref[ri_ref[4 * t + k]] + ri_ref[4 * t + 2 + k]


def _dispatch_body(sp_ref, ri_ref, x_ref, g_ref, m_ref, xp_in, xp_hbm, slab, sem, *, tm, sub, n_steps):
    del xp_in
    i = pl.program_id(0)
    slot = lax.rem(i, 2)

    def drain(s):
        def body(t, carry):
            for k in (0, 1):
                _slab_copy(slab.at[s], 0, xp_hbm, 0, sub, sem.at[s]).wait()
            return carry

        lax.fori_loop(0, tm, body, 0)

    @pl.when(i >= 2)
    def _():
        drain(slot)

    h = _norm_mod(x_ref[...], g_ref[...], m_ref, 3, 4)
    for c in range(sub):
        slab[slot, pl.ds(c, tm, stride=sub), :] = h[:, c * 128:(c + 1) * 128]

    def issue(t, carry):
        for k in (0, 1):
            _slab_copy(slab.at[slot], t, xp_hbm, _slot_row(sp_ref, ri_ref, t, k), sub, sem.at[slot]).start()
        return carry

    lax.fori_loop(0, tm, issue, 0)

    @pl.when(i == n_steps - 1)
    def _():
        if n_steps >= 2:
            drain(1 - slot)
        drain(slot)


def _dispatch(starts_p, ri_flat, x, g_all, gi, mod, layer, modidx, n_rows, tm):
    n, d = x.shape
    sub = d // 128
    xp0 = jnp.zeros((n_rows * sub, 128), F32)
    return pl.pallas_call(
        functools.partial(_dispatch_body, tm=tm, sub=sub, n_steps=n // tm),
        out_shape=_sds(xp0.shape, xp0.dtype),
        grid_spec=pltpu.PrefetchScalarGridSpec(
            num_scalar_prefetch=1,
            grid=(n // tm,),
            in_specs=[pl.BlockSpec((4 * tm,), lambda i, sp: (i,), memory_space=pltpu.SMEM),
                      pl.BlockSpec((tm, d), lambda i, sp: (i, 0)),
                      pl.BlockSpec((None, 1, d), lambda i, sp: (gi, 0, 0)),
                      _mod_spec(layer, modidx, d),
                      pl.BlockSpec(memory_space=pl.ANY)],
            out_specs=pl.BlockSpec(memory_space=pl.ANY),
            scratch_shapes=[pltpu.VMEM((2, tm * sub, 128), F32), pltpu.SemaphoreType.DMA((2,))]),
        input_output_aliases={5: 0},
        compiler_params=_cp(("arbitrary",)),
        name="moe_dispatch",
    )(starts_p, ri_flat, x, g_all, mod, xp0)


def _expert_body(be_ref, nu_ref, x_ref, w1_ref, w3_ref, w2_ref, o_ref, w1b, w3b, w2b, *, rows, sub):
    b = pl.program_id(0)
    used = b < nu_ref[0]

    @pl.when(used & ((b == 0) | (be_ref[b] != be_ref[jnp.maximum(b - 1, 0)])))
    def _():
        w1b[...] = w1_ref[...].astype(BF16)
        w3b[...] = w3_ref[...].astype(BF16)
        w2b[...] = w2_ref[...].astype(BF16)

    @pl.when(used)
    def _():
        xb = jnp.concatenate([x_ref[pl.ds(c, rows, stride=sub), :] for c in range(sub)], axis=1).astype(BF16)
        a = jnp.dot(xb, w1b[...], preferred_element_type=F32)
        g = jnp.dot(xb, w3b[...], preferred_element_type=F32)
        y = jnp.dot((_silu(a) * g).astype(BF16), w2b[...], preferred_element_type=F32)
        for c in range(sub):
            o_ref[pl.ds(c, rows, stride=sub), :] = y[:, c * 128:(c + 1) * 128]

    @pl.when(jnp.logical_not(used))
    def _():
        o_ref[...] = jnp.zeros_like(o_ref)


def _experts(blk_e, n_used, xp, w1, w3, w2, layer):
    d, de = w1.shape[2], w1.shape[3]
    sub = d // 128
    n_rows = xp.shape[0] // sub
    return pl.pallas_call(
        functools.partial(_expert_body, rows=MOE_ROWS, sub=sub),
        out_shape=_sds(xp.shape, F32),
        grid_spec=pltpu.PrefetchScalarGridSpec(
            num_scalar_prefetch=2,
            grid=(n_rows // MOE_ROWS,),
            in_specs=[pl.BlockSpec((MOE_ROWS * sub, 128), lambda b, be, nu: (b, 0)),
                      pl.BlockSpec((None, None, d, de), lambda b, be, nu: (layer, be[b], 0, 0)),
                      pl.BlockSpec((None, None, d, de), lambda b, be, nu: (layer, be[b], 0, 0)),
                      pl.BlockSpec((None, None, de, d), lambda b, be, nu: (layer, be[b], 0, 0))],
            out_specs=pl.BlockSpec((MOE_ROWS * sub, 128), lambda b, be, nu: (b, 0)),
            scratch_shapes=[pltpu.VMEM((d, de), BF16), pltpu.VMEM((d, de), BF16), pltpu.VMEM((de, d), BF16)]),
        compiler_params=_cp(("arbitrary",)),
        name="moe_experts",
    )(blk_e, n_used, xp, w1, w3, w2)


def _combine_body(sp_ref, ri_ref, rin_ref, x_ref, rw_ref, m_ref, yp_hbm, o_ref, slab, sem, *, tm, sub, n_steps):
    i = pl.program_id(0)
    slot = lax.rem(i, 2)

    def gather(idx_ref, s):
        def body(t, carry):
            for k in (0, 1):
                _slab_copy(yp_hbm, _slot_row(sp_ref, idx_ref, t, k), slab.at[s], 2 * t + k, sub, sem.at[s]).start()
            return carry

        lax.fori_loop(0, tm, body, 0)

    @pl.when(i == 0)
    def _():
        gather(ri_ref, 0)

    @pl.when(i + 1 < n_steps)
    def _():
        gather(rin_ref, 1 - slot)

    def drain(t, carry):
        for k in (0, 1):
            _slab_copy(yp_hbm, 0, slab.at[slot], 0, sub, sem.at[slot]).wait()
        return carry

    lax.fori_loop(0, tm, drain, 0)
    rw = rw_ref[...]
    w1, w2 = rw[:, 0:1], rw[:, 1:2]
    for c in range(sub):
        sl = slice(c * 128, (c + 1) * 128)
        y = slab[slot, pl.ds(c, tm, stride=2 * sub), :] * w1 + slab[slot, pl.ds(sub + c, tm, stride=2 * sub), :] * w2
        o_ref[:, sl] = x_ref[:, sl] + m_ref[5:6, sl] * y


def _combine(starts_p, ri_flat, x, rw, mod, layer, yp, modidx, tm):
    n, d = x.shape
    sub = d // 128
    n_steps = n // tm
    return pl.pallas_call(
        functools.partial(_combine_body, tm=tm, sub=sub, n_steps=n_steps),
        out_shape=_sds((n, d), F32),
        grid_spec=pltpu.PrefetchScalarGridSpec(
            num_scalar_prefetch=1,
            grid=(n_steps,),
            in_specs=[pl.BlockSpec((4 * tm,), lambda i, sp: (i,), memory_space=pltpu.SMEM),
                      pl.BlockSpec((4 * tm,), lambda i, sp: (jnp.minimum(i + 1, n_steps - 1),), memory_space=pltpu.SMEM),
                      pl.BlockSpec((tm, d), lambda i, sp: (i, 0)),
                      pl.BlockSpec((tm, 128), lambda i, sp: (i, 0)),
                      _mod_spec(layer, modidx, d),
                      pl.BlockSpec(memory_space=pl.ANY)],
            out_specs=pl.BlockSpec((tm, d), lambda i, sp: (i, 0)),
            scratch_shapes=[pltpu.VMEM((2, 2 * tm * sub, 128), F32), pltpu.SemaphoreType.DMA((2,))]),
        compiler_params=_cp(("arbitrary",)),
        name="moe_combine",
    )(starts_p, ri_flat, ri_flat, x, rw, mod, yp)


def _router_weights(wg, bg, we, be):
    pad = 128 - N_GROUPS - N_EXPERTS
    wr = jnp.pad(jnp.concatenate([wg, we], axis=2), ((0, 0), (0, 0), (0, pad)))
    wr_hi = wr.astype(BF16)
    wr_lo = (wr - wr_hi.astype(F32)).astype(BF16)
    br = jnp.pad(jnp.concatenate([bg, be], axis=1), ((0, 0), (0, pad)))[:, None, :]
    return wr_hi, wr_lo, br


def _moe(x, g_all, gi, mod, layer, wr_hi, wr_lo, br, w1, w3, w2, modidx):
    n, d = x.shape
    tm = 256
    ri, rw, cnt = _router(x, g_all, gi, mod, layer, wr_hi, wr_lo, br, modidx, tm)
    counts = cnt[0, :N_EXPERTS].astype(jnp.int32)
    padded = (counts + MOE_ROWS - 1) // MOE_ROWS * MOE_ROWS
    ends_p = jnp.cumsum(padded)
    starts_p = (ends_p - padded).astype(jnp.int32)
    n_blocks = (2 * n) // MOE_ROWS + N_EXPERTS
    blk_start = jnp.arange(n_blocks, dtype=jnp.int32) * MOE_ROWS
    blk_e = jnp.minimum(jnp.sum((ends_p[None, :] <= blk_start[:, None]).astype(jnp.int32), axis=1), N_EXPERTS - 1)
    n_used = (ends_p[-1:] // MOE_ROWS).astype(jnp.int32)
    ri_flat = ri[:, :4].reshape(-1)
    xp = _dispatch(starts_p, ri_flat, x, g_all, gi, mod, layer, modidx, n_blocks * MOE_ROWS, tm)
    yp = _experts(blk_e, n_used, xp, w1, w3, w2, layer)
    return _combine(starts_p, ri_flat, x, rw, mod, layer, yp, modidx, tm)


def _final_norm_body(x_ref, g_ref, o_ref):
    x = x_ref[...]
    o_ref[...] = x * lax.rsqrt(jnp.mean(x * x, axis=-1, keepdims=True) + EPS) * g_ref[...]


def _final_norm(x, g, row0, n_rows, tm):
    d = x.shape[1]
    rb = row0 // tm
    return pl.pallas_call(
        _final_norm_body,
        out_shape=_sds((n_rows, d), F32),
        grid=(n_rows // tm,),
        in_specs=[pl.BlockSpec((tm, d), lambda i: (rb + i, 0)), pl.BlockSpec((1, d), lambda i: (0, 0))],
        out_specs=pl.BlockSpec((tm, d), lambda i: (i, 0)),
        compiler_params=_cp(("parallel",)),
        name="final_norm",
    )(x, g.reshape(1, d))


def kernel(x_prompt, x_sample, c, state_ret, state_gdn, cache_k, cache_v, c_ctx, w_mod, b_mod, norm_g, even_w_in, even_w_out, ret_decay_logit, ret_gn_w, gdn_conv_w, gdn_a_log, gdn_dt_bias, gdn_norm_w, na_w_in, na_w_out, na_rpb, moe_wg, moe_bg, moe_we, moe_be, moe_w1, moe_w3, moe_w2, final_norm_g):
    bp, seq, d = x_prompt.shape
    bs, dseq, _ = x_sample.shape
    depth = w_mod.shape[0]
    h_a, dk_a = state_ret.shape[3], state_ret.shape[4]
    h_b, dk_b = state_gdn.shape[3], state_gdn.shape[4]
    h_c, dh_c = cache_k.shape[3], cache_k.shape[4]
    a_qk = h_a * dk_a
    b_qk = h_b * dk_b
    n_main = 4 * a_qk + 4 * b_qk
    np_rows = bp * seq
    n = np_rows + bs * dseq
    assert bs + 1 <= 8 and dseq % GRID_W == 0 and dseq // GRID_W >= KH and 4 * h_b <= 128
    assert np_rows % dseq == 0 and dseq % seq == 0

    tm_proj, tn_proj = 1024, 1024
    tm_out = 512
    modidx_proj = _mod_index(tm_proj, np_rows, dseq)
    modidx_out = _mod_index(tm_out, np_rows, dseq)
    modidx256 = _mod_index(256, np_rows, dseq)
    assert np_rows % tm_proj == 0 and dseq % tm_proj == 0

    cvec = jnp.concatenate([c_ctx[None, :], c, jnp.zeros((8 - 1 - bs, d), F32)], axis=0)
    mod = _adaln(cvec, w_mod, b_mod).reshape(depth, 8, 6, d)
    x = jnp.concatenate([x_prompt.reshape(np_rows, d), x_sample.reshape(bs * dseq, d)], axis=0)
    cos, sin = _rope_tables(dseq, dk_a)

    g_all = norm_g.reshape(depth * 2, 1, d)
    w_main = even_w_in[:, :, :n_main].astype(BF16)
    w_small = jnp.pad(even_w_in[:, :, n_main:], ((0, 0), (0, 0), (0, 128 - 4 * h_b))).astype(BF16)
    n_even = even_w_in.shape[0]
    gp = jnp.stack([jnp.pad(gdn_a_log.reshape(n_even, -1), ((0, 0), (0, 128 - 2 * h_b))),
                    jnp.pad(gdn_dt_bias.reshape(n_even, -1), ((0, 0), (0, 128 - 2 * h_b)))], axis=1).astype(F32)
    log_gamma = jax.nn.log_sigmoid(ret_decay_logit.astype(F32)).reshape(n_even * 2, h_a)
    gn_w = ret_gn_w.reshape(n_even, 1, h_a * dk_a)
    gdn_nw = gdn_norm_w.reshape(n_even, 1, dk_b)
    w_out_even = even_w_out.astype(BF16)
    w_in_odd = na_w_in.astype(BF16)
    w_out_odd = na_w_out.astype(BF16)
    bias = _na_bias_table(na_rpb)
    wr_hi, wr_lo, br = _router_weights(moe_wg, moe_bg, moe_we, moe_be)

    new_ret, new_gdn, new_k, new_v = [], [], [], []
    for l in range(depth):
        i = l // 2
        if l % 2 == 0:
            p, ps = _proj(x, g_all, 2 * l, mod, l, w_main, i, modidx_proj, tm_proj, tn_proj, small=(w_small, gp, i), n_h=h_b)
            a_p, sr = _retention(p, log_gamma, gn_w, i, 0, bp, seq, False, h_a, dk_a)
            (a_s,) = _retention(p, log_gamma, gn_w, i, np_rows, bs, dseq, True, h_a, dk_a, cos, sin, state_ret)
            b_p, sg = _gdn(p, ps, gdn_conv_w, gdn_nw, i, 4 * a_qk, 0, bp, seq, False, h_b, dk_b)
            (b_s,) = _gdn(p, ps, gdn_conv_w, gdn_nw, i, 4 * a_qk, np_rows, bs, dseq, True, h_b, dk_b, state_gdn)
            new_ret.append(sr)
            new_gdn.append(sg)
            mix_a = jnp.concatenate([a_p, a_s], axis=0)
            mix_b = jnp.concatenate([b_p, b_s], axis=0)
            x = _oproj([mix_a, mix_b], w_out_even, i, x, mod, l, modidx_out, 2, tm_out, d)
        else:
            d_c = h_c * dh_c
            p = _proj(x, g_all, 2 * l, mod, l, w_in_odd, i, modidx_proj, tm_proj, tn_proj)
            o_p = _ctx_attn(p, bp, seq, h_c, dh_c)
            o_s = _na_attn(p, cache_k, cache_v, i, bias, np_rows, bs, dseq, h_c, dh_c)
            new_k.append(p[:np_rows, d_c:2 * d_c].reshape(bp, seq, h_c, dh_c))
            new_v.append(p[:np_rows, 2 * d_c:].reshape(bp, seq, h_c, dh_c))
            x = _oproj([jnp.concatenate([o_p, o_s], axis=0)], w_out_odd, i, x, mod, l, modidx_out, 2, tm_out, d)
        x = _moe(x, g_all, 2 * l + 1, mod, l, wr_hi, wr_lo, br, moe_w1, moe_w3, moe_w2, modidx256)

    y_p = _final_norm(x, final_norm_g, 0, np_rows, tm_out)
    y_s = _final_norm(x, final_norm_g, np_rows, bs * dseq, tm_out)
    return (y_p.reshape(bp, seq, d), y_s.reshape(bs, dseq, d),
            jnp.stack(new_ret, axis=1), jnp.stack(new_gdn, axis=1), jnp.stack(new_k, axis=1), jnp.stack(new_v, axis=1))
```

```python
import functools

import numpy as np
import jax
import jax.numpy as jnp
from jax import lax
from jax.experimental import pallas as pl
from jax.experimental.pallas import tpu as pltpu

F32 = jnp.float32
BF16 = jnp.bfloat16
EPS = 1e-6
NEG = -1e30
CHUNK = 128
GRID_W = 64
KH = 8
KW = 16
ROPE_BASE = 10000.0
N_GROUPS = 4
EXP_PER_GROUP = 8
N_EXPERTS = N_GROUPS * EXP_PER_GROUP
MOE_ROWS = 256
VMEM_LIMIT = 56 * 1024 * 1024


def _sds(shape, dtype):
    return jax.ShapeDtypeStruct(shape, dtype)


def _cp(sem, vmem=VMEM_LIMIT):
    return pltpu.CompilerParams(dimension_semantics=sem, vmem_limit_bytes=vmem)


def _bdot(a, b):
    return jnp.dot(a.astype(BF16), b.astype(BF16), preferred_element_type=F32)


def _bdot_nt(a, b):
    return lax.dot_general(a.astype(BF16), b.astype(BF16), (((1,), (1,)), ((), ())), preferred_element_type=F32)


def _bdot_tn(a, b):
    return lax.dot_general(a.astype(BF16), b.astype(BF16), (((0,), (0,)), ((), ())), preferred_element_type=F32)


def _split2(a):
    hi = a.astype(BF16)
    lo = (a - hi.astype(F32)).astype(BF16)
    return hi, lo


def _blk(idx, size):
    return jnp.right_shift(idx, int(np.log2(size)))


def _silu(x):
    return x / (1.0 + jnp.exp(-x))


def _sigmoid(x):
    return 1.0 / (1.0 + jnp.exp(-x))


def _softplus(x):
    return jnp.maximum(x, 0.0) + jnp.log(1.0 + jnp.exp(-jnp.abs(x)))


def _mod_index(tm, n_prompt_rows, rows_per_sample):
    npt = n_prompt_rows // tm
    per = rows_per_sample // tm

    def f(i):
        return jnp.where(i < npt, 0, 1 + (i - npt) // per)

    return f


def _norm_mod(x, g, m_ref, shift_row, scale_row):
    r = lax.rsqrt(jnp.mean(x * x, axis=-1, keepdims=True) + EPS)
    y = x * r * g
    return y * (1.0 + m_ref[scale_row:scale_row + 1, :]) + m_ref[shift_row:shift_row + 1, :]


def _adaln_body(c_ref, w_ref, b_ref, o_ref):
    s = _silu(c_ref[...])
    o_ref[...] = _bdot(s, w_ref[...]) + b_ref[...]


def _adaln(cvec, w_mod, b_mod):
    n_l, d, d6 = w_mod.shape
    tn = 1024
    return pl.pallas_call(
        _adaln_body,
        out_shape=_sds((n_l, 8, d6), F32),
        grid=(n_l, d6 // tn),
        in_specs=[pl.BlockSpec((8, d), lambda l, j: (0, 0)),
                  pl.BlockSpec((None, d, tn), lambda l, j: (l, 0, j)),
                  pl.BlockSpec((None, 1, tn), lambda l, j: (l, 0, j))],
        out_specs=pl.BlockSpec((None, 8, tn), lambda l, j: (l, 0, j)),
        compiler_params=_cp(("parallel", "parallel")),
        name="adaln",
    )(cvec, w_mod, b_mod.reshape(n_l, 1, d6))


def _proj_body(x_ref, g_ref, m_ref, w_ref, *rest, gates, n_h):
    if gates:
        ws_ref, gp_ref, o_ref, os_ref, h_scr = rest
    else:
        o_ref, h_scr = rest

    @pl.when(pl.program_id(1) == 0)
    def _():
        rows = 256
        for r0 in range(0, x_ref.shape[0], rows):
            rs = slice(r0, r0 + rows)
            hb = _norm_mod(x_ref[rs, :], g_ref[...], m_ref, 0, 1).astype(BF16)
            h_scr[rs, :] = hb
            if gates:
                os_ref[rs, :] = _gdn_gates(jnp.dot(hb, ws_ref[...], preferred_element_type=F32), gp_ref, n_h)

    o_ref[...] = jnp.dot(h_scr[...], w_ref[...], preferred_element_type=F32)


def _gdn_gates(ps, gp_ref, n_h):
    lane = lax.broadcasted_iota(jnp.int32, ps.shape, 1)
    g = -jnp.exp(gp_ref[0:1, :]) * _softplus(ps + gp_ref[1:2, :])
    ii = lax.broadcasted_iota(jnp.int32, (CHUNK, CHUNK), 0)
    jj = lax.broadcasted_iota(jnp.int32, (CHUNK, CHUNK), 1)
    pre = jnp.where(jj <= ii, 1.0, 0.0).astype(BF16)
    suf = jnp.where(jj >= ii, 1.0, 0.0).astype(BF16)

    def tri_dot(m, v):
        v1 = v.astype(BF16)
        r1 = v - v1.astype(F32)
        v2 = r1.astype(BF16)
        v3 = (r1 - v2.astype(F32)).astype(BF16)
        d = functools.partial(jnp.dot, preferred_element_type=F32)
        return d(m, v1) + d(m, v2) + d(m, v3)

    g_f = jnp.where(lane < n_h, g, 0.0)
    g_b = jnp.where((lane >= n_h) & (lane < 2 * n_h), g, 0.0)
    gc = jnp.concatenate([tri_dot(pre, g_f[c0:c0 + CHUNK]) + tri_dot(suf, g_b[c0:c0 + CHUNK])
                          for c0 in range(0, ps.shape[0], CHUNK)], axis=0)
    return jnp.where(lane < 2 * n_h, gc, jnp.where(lane < 4 * n_h, _sigmoid(ps), 0.0))


def _mod_spec(layer, modidx, width, col=None):
    def tail(i, *rest):
        return (modidx(i), 0, 0 if col is None else col(i, *rest))

    return pl.BlockSpec((None, None, 6, width), lambda *idx: (layer,) + tail(*idx))


def _proj(x, g_all, gi, mod, layer, w_all, wi, modidx, tm, tn, small=None, n_h=0):
    n, d = x.shape
    nw = w_all.shape[2]
    in_specs = [pl.BlockSpec((tm, d), lambda i, j: (i, 0)),
                pl.BlockSpec((None, 1, d), lambda i, j: (gi, 0, 0)),
                _mod_spec(layer, modidx, d),
                pl.BlockSpec((None, d, tn), lambda i, j: (wi, 0, j))]
    args = [x, g_all, mod, w_all]
    out_shape = [_sds((n, nw), F32)]
    out_specs = [pl.BlockSpec((tm, tn), lambda i, j: (i, j))]
    if small is not None:
        ws, gp, si = small
        in_specs += [pl.BlockSpec((None, d, 128), lambda i, j: (si, 0, 0)), pl.BlockSpec((None, 2, 128), lambda i, j: (si, 0, 0))]
        args += [ws, gp]
        out_shape.append(_sds((n, 128), F32))
        out_specs.append(pl.BlockSpec((tm, 128), lambda i, j: (i, 0)))
    res = pl.pallas_call(
        functools.partial(_proj_body, gates=small is not None, n_h=n_h),
        out_shape=out_shape,
        grid=(n // tm, nw // tn),
        in_specs=in_specs,
        out_specs=out_specs,
        scratch_shapes=[pltpu.VMEM((tm, d), BF16)],
        compiler_params=_cp(("parallel", "arbitrary")),
        name="proj",
    )(*args)
    return res if small is not None else res[0]


def _oproj_body(*refs, n_a, gate_row, npt):
    ap_refs = refs[:n_a]
    as_refs = refs[n_a:2 * n_a]
    w_refs = refs[2 * n_a:3 * n_a]
    x_ref, m_ref, o_ref = refs[3 * n_a:]

    def run(a_refs):
        y = jnp.dot(a_refs[0][...], w_refs[0][...], preferred_element_type=F32)
        for a_ref, w_ref in zip(a_refs[1:], w_refs[1:]):
            y = y + jnp.dot(a_ref[...], w_ref[...], preferred_element_type=F32)
        o_ref[...] = x_ref[...] + m_ref[gate_row:gate_row + 1, :] * y

    @pl.when(pl.program_id(0) < npt)
    def _():
        run(ap_refs)

    @pl.when(pl.program_id(0) >= npt)
    def _():
        run(as_refs)


def _oproj(ap_list, as_list, w_all, wi, x, mod, layer, modidx, gate_row, tm, tn):
    n, d = x.shape
    n_a = len(ap_list)
    npt = ap_list[0].shape[0] // tm
    assert ap_list[0].shape[0] % tm == 0 and as_list[0].shape[0] % tm == 0
    in_specs = [pl.BlockSpec((tm, a.shape[1]), lambda i, j: (jnp.minimum(i, npt - 1), 0)) for a in ap_list]
    in_specs += [pl.BlockSpec((tm, a.shape[1]), lambda i, j: (jnp.maximum(i - npt, 0), 0)) for a in as_list]
    k0 = 0
    for a in ap_list:
        ka = a.shape[1]
        assert k0 % ka == 0
        in_specs.append(pl.BlockSpec((None, ka, tn), lambda i, j, kb=k0 // ka: (wi, kb, j)))
        k0 += ka
    in_specs += [pl.BlockSpec((tm, tn), lambda i, j: (i, j)),
                 _mod_spec(layer, modidx, tn, col=lambda i, j: j)]
    return pl.pallas_call(
        functools.partial(_oproj_body, n_a=n_a, gate_row=gate_row, npt=npt),
        out_shape=_sds((n, d), F32),
        grid=(n // tm, d // tn),
        in_specs=in_specs,
        out_specs=pl.BlockSpec((tm, tn), lambda i, j: (i, j)),
        compiler_params=_cp(("parallel", "arbitrary")),
        name="oproj",
    )(*ap_list, *as_list, *([w_all] * n_a), x, mod)


def _ret_body(lg_ref, q_ref, k_ref, v_ref, ga_ref, gw_ref, *rest, t_len, tq, latent, dk, li):
    if latent:
        cos_ref, sin_ref, s0_ref, o_ref = rest
    else:
        o_ref, st_ref = rest
    h = pl.program_id(1)
    lgf = lg_ref[2 * li, h]
    lgb = lg_ref[2 * li + 1, h]
    half = dk // 2

    def rope(x, r0, rows):
        if not latent:
            return x
        xs = jnp.concatenate([pltpu.roll(x[:, :half], half // 2, axis=1), pltpu.roll(x[:, half:], half // 2, axis=1)], axis=1)
        return x * cos_ref[r0:r0 + rows, :] + xs * sin_ref[r0:r0 + rows, :]

    k = rope(k_ref[...], 0, t_len)
    kb = k.astype(BF16)
    vb = v_ref[...].astype(BF16)
    if latent:
        s0f = s0_ref[0].astype(BF16)
        s0b = s0_ref[1].astype(BF16)
    dd0 = (lax.broadcasted_iota(jnp.int32, (tq, tq), 0) - lax.broadcasted_iota(jnp.int32, (tq, tq), 1)).astype(F32)
    for qi in range(t_len // tq):
        r0 = qi * tq
        q = rope(q_ref[r0:r0 + tq, :], r0, tq) * (dk ** -0.5)
        s = _bdot_nt(q, kb)
        pieces = []
        for kj in range(t_len // tq):
            dd = dd0 + float((qi - kj) * tq)
            if kj < qi:
                pieces.append(jnp.exp(lgf * dd))
            elif kj > qi:
                pieces.append(jnp.exp(lgb * (-dd)))
            else:
                pieces.append(jnp.where(dd >= 0, jnp.exp(lgf * jnp.maximum(dd, 0.0)), 0.0)
                              + jnp.where(dd <= 0, jnp.exp(lgb * jnp.maximum(-dd, 0.0)), 0.0))
        dec = pieces[0] if len(pieces) == 1 else jnp.concatenate(pieces, axis=1)
        o = _bdot(s * dec, vb)
        if latent:
            pos = (lax.broadcasted_iota(jnp.int32, (tq, 1), 0) + r0).astype(F32)
            o = o + _bdot(q * jnp.exp(lgf * (pos + 1.0)), s0f) + _bdot(q * jnp.exp(lgb * (t_len - pos)), s0b)
        o = o - jnp.mean(o, axis=-1, keepdims=True)
        o = o * lax.rsqrt(jnp.mean(o * o, axis=-1, keepdims=True) + EPS)
        o_ref[r0:r0 + tq, :] = (_silu(ga_ref[r0:r0 + tq, :]) * (o * gw_ref[...])).astype(o_ref.dtype)
    if not latent:
        pos = lax.broadcasted_iota(jnp.int32, (t_len, 1), 0).astype(F32)
        st_ref[0] = _bdot_tn(k * jnp.exp(lgf * (t_len - 1.0 - pos)), vb)
        st_ref[1] = _bdot_tn(k * jnp.exp(lgb * pos), vb)


def _retention(p, log_gamma, gn_w, li, row0, n_seq, t_len, latent, h_a, dk, cos=None, sin=None, s0=None):
    rb = row0 // t_len
    in_specs = [pl.BlockSpec(memory_space=pltpu.SMEM),
                pl.BlockSpec((t_len, dk), lambda b, h: (rb + b, h)),
                pl.BlockSpec((t_len, dk), lambda b, h: (rb + b, h_a + h)),
                pl.BlockSpec((t_len, dk), lambda b, h: (rb + b, 2 * h_a + h)),
                pl.BlockSpec((t_len, dk), lambda b, h: (rb + b, 3 * h_a + h)),
                pl.BlockSpec((None, 1, dk), lambda b, h: (li, 0, h))]
    args = [log_gamma, p, p, p, p, gn_w]
    out_shape = [_sds((n_seq * t_len, h_a * dk), BF16)]
    out_specs = [pl.BlockSpec((t_len, dk), lambda b, h: (b, h))]
    if latent:
        in_specs += [pl.BlockSpec((t_len, dk), lambda b, h: (0, 0)),
                     pl.BlockSpec((t_len, dk), lambda b, h: (0, 0)),
                     pl.BlockSpec((None, None, 2, None, dk, dk), lambda b, h: (b, li, 0, h, 0, 0))]
        args += [cos, sin, s0]
    else:
        out_shape.append(_sds((n_seq, 2, h_a, dk, dk), F32))
        out_specs.append(pl.BlockSpec((None, 2, None, dk, dk), lambda b, h: (b, 0, h, 0, 0)))
    res = pl.pallas_call(
        functools.partial(_ret_body, t_len=t_len, tq=min(t_len, 256), latent=latent, dk=dk, li=li),
        out_shape=out_shape,
        grid=(n_seq, h_a),
        in_specs=in_specs,
        out_specs=out_specs,
        compiler_params=_cp(("parallel", "parallel")),
        name="retention_latent" if latent else "retention_ctx",
    )(*args)
    return res


def _rope_tables(t_len, dk):
    nf = dk // 4
    pos = jnp.arange(t_len)
    inv = ROPE_BASE ** (-jnp.arange(nf, dtype=F32) / nf)
    ang_r = (pos // GRID_W).astype(F32)[:, None] * inv
    ang_c = (pos % GRID_W).astype(F32)[:, None] * inv
    cr, sr, cc, sc = jnp.cos(ang_r), jnp.sin(ang_r), jnp.cos(ang_c), jnp.sin(ang_c)
    cos = jnp.concatenate([cr, cr, cc, cc], axis=1).astype(F32)
    sin = jnp.concatenate([-sr, sr, -sc, sc], axis=1).astype(F32)
    return cos, sin


def _tri_solve_many(lms, rhss, ii, jj):
    eye = jnp.where(ii == jj, 1.0, 0.0)
    diag16 = _blk(ii, 16) == _blk(jj, 16)
    d0 = [jnp.where(diag16, lm, 0.0) for lm in lms]
    t = [eye - d for d in d0]
    p = [_bdot(d, d) for d in d0]
    for stage in range(3):
        t = [ti + _bdot(ti, pi) for ti, pi in zip(t, p)]
        if stage < 2:
            p = [_bdot(pi, pi) for pi in p]
    for s in (16, 32, 64):
        off = (_blk(ii, 2 * s) == _blk(jj, 2 * s)) & (_blk(ii, s) != _blk(jj, s))
        te = [_bdot(ti, jnp.where(off, lm, 0.0)) for ti, lm in zip(t, lms)]
        t = [ti - _bdot(tei, ti) for ti, tei in zip(t, te)]
    return [_bdot(ti, ri) for ti, ri in zip(t, rhss)]


def _gdn_chunk_masks(kk, gcol, bcol, rev, ii, jj):
    c = kk.shape[0]
    gb = jnp.broadcast_to(gcol, (c, c))
    diff = gb - gb.T
    incl = (ii <= jj) if rev else (ii >= jj)
    strict = (ii < jj) if rev else (ii > jj)
    decay = jnp.where(incl, jnp.exp(jnp.where(incl, diff, 0.0)), 0.0)
    return decay, jnp.where(strict, kk * decay, 0.0) * bcol


def _gdn_body(q_ref, k_ref, v_ref, z_ref, ps_ref, cq_ref, ck_ref, cv_ref, nw_ref, *rest, t_len, latent, n_h, dk, hb, cpg):
    if latent:
        s0_ref, o_ref, qs, ks, vs, oacc, u_s, w_s, a_s, qg_s, kt_s, gl_s = rest
    else:
        o_ref, st_ref, qs, ks, vs, oacc, u_s, w_s, a_s, qg_s, kt_s, gl_s = rest
    h0 = pl.program_id(1) * hb
    row = lax.broadcasted_iota(jnp.int32, (t_len, hb * dk), 0)

    def conv_silu(x_ref, w_ref):
        x = x_ref[...]
        acc = w_ref[2:3, :] * x
        for j in (0, 1, 3, 4):
            s = 2 - j
            xs = pltpu.roll(x, s % t_len, axis=0)
            ok = (row - s >= 0) & (row - s < t_len)
            acc = acc + w_ref[j:j + 1, :] * jnp.where(ok, xs, 0.0)
        return _silu(acc)

    def l2norm_heads(x, scale):
        parts = []
        for hl in range(hb):
            xh = x[:, hl * dk:(hl + 1) * dk]
            parts.append(xh * lax.rsqrt(jnp.sum(xh * xh, axis=-1, keepdims=True) + EPS) * scale)
        return parts[0] if hb == 1 else jnp.concatenate(parts, axis=1)

    qs[...] = l2norm_heads(conv_silu(q_ref, cq_ref), dk ** -0.5)
    ks[...] = l2norm_heads(conv_silu(k_ref, ck_ref), 1.0)
    vs[...] = conv_silu(v_ref, cv_ref)
    oacc[...] = jnp.zeros_like(oacc)

    n_c = t_len // CHUNK
    ii = lax.broadcasted_iota(jnp.int32, (CHUNK, CHUNK), 0)
    jj = lax.broadcasted_iota(jnp.int32, (CHUNK, CHUNK), 1)
    lane = lax.broadcasted_iota(jnp.int32, (CHUNK, 128), 1)

    def rows_of(c):
        return pl.ds(c * CHUNK if isinstance(c, int) else pl.multiple_of(c * CHUNK, CHUNK), CHUNK)

    def prepare_group(c0):
        pairs = [(hl, c0 + j) for hl in range(hb) for j in range(cpg)]
        qc = [qs[rows_of(c), hl * dk:(hl + 1) * dk] for hl, c in pairs]
        kc = [ks[rows_of(c), hl * dk:(hl + 1) * dk] for hl, c in pairs]
        vc = [vs[rows_of(c), hl * dk:(hl + 1) * dk] for hl, c in pairs]
        kk = [_bdot_nt(k, k) for k in kc]
        qk = [_bdot_nt(q, k) for q, k in zip(qc, kc)]
        probs, decays, lms, rhss, gcols = [], [], [], [], []
        for i, (hl, c) in enumerate(pairs):
            ps = ps_ref[rows_of(c), :]
            for d in (0, 1):
                gcol = jnp.sum(jnp.where(lane == d * n_h + h0 + hl, ps, 0.0), axis=1, keepdims=True)
                bcol = jnp.sum(jnp.where(lane == (2 + d) * n_h + h0 + hl, ps, 0.0), axis=1, keepdims=True)
                decay, lm = _gdn_chunk_masks(kk[i], gcol, bcol, d == 1, ii, jj)
                probs.append((i, hl, c, d))
                decays.append(decay)
                lms.append(lm)
                gcols.append(gcol)
                rhss.append(jnp.concatenate([vc[i] * bcol, kc[i] * (bcol * jnp.exp(gcol))], axis=1))
        sols = _tri_solve_many(lms, rhss, ii, jj)
        for (i, hl, c, d), decay, gcol, sol in zip(probs, decays, gcols, sols):
            gtot = gcol[0:1, :] if d == 1 else gcol[CHUNK - 1:CHUNK, :]
            idx = (hl * 2 + d) * n_c + c
            u_s[idx] = sol[:, :dk]
            w_s[idx] = sol[:, dk:].astype(BF16)
            a_s[idx] = (qk[i] * decay).astype(BF16)
            qg_s[idx] = (qc[i] * jnp.exp(gcol)).astype(BF16)
            kt_s[idx] = (kc[i] * jnp.exp(gtot - gcol)).T.astype(BF16)
            gl_s[idx] = jnp.broadcast_to(jnp.exp(gtot), (8, 128))

    if n_c == cpg:
        prepare_group(0)
    else:
        def group_step(it, carry):
            prepare_group(it * cpg)
            return carry

        lax.fori_loop(0, n_c // cpg, group_step, 0)

    chains = [(hl, d) for hl in range(hb) for d in (0, 1)]

    def step(it, states):
        idx = [(hl * 2 + d) * n_c + (n_c - 1 - it if d == 1 else it) for hl, d in chains]
        sb16 = [s.astype(BF16) for s in states]
        dot = functools.partial(jnp.dot, preferred_element_type=F32)
        v_new = [(u_s[i] - dot(w_s[i], s)).astype(BF16) for i, s in zip(idx, sb16)]
        o_st = [dot(qg_s[i], s) for i, s in zip(idx, sb16)]
        o_in = [dot(a_s[i], v) for i, v in zip(idx, v_new)]
        upd = [dot(kt_s[i], v) for i, v in zip(idx, v_new)]
        for (hl, d), a, b in zip(chains, o_st, o_in):
            c = n_c - 1 - it if d == 1 else it
            oacc[rows_of(c), hl * dk:(hl + 1) * dk] += a + b
        return tuple(s * gl_s[i][0:1, :] + x for s, i, x in zip(states, idx, upd))

    if latent:
        init = tuple(s0_ref[d, hl] for hl, d in chains)
    else:
        init = tuple(jnp.zeros((dk, dk), F32) for _ in chains)
    fin = lax.fori_loop(0, n_c, step, init)
    if not latent:
        for (hl, d), s in zip(chains, fin):
            st_ref[d, hl] = s
    u = oacc[...]
    z = z_ref[...]
    for hl in range(hb):
        sl = slice(hl * dk, (hl + 1) * dk)
        uh = u[:, sl]
        uh = uh * lax.rsqrt(jnp.mean(uh * uh, axis=-1, keepdims=True) + EPS) * nw_ref[...]
        o_ref[:, sl] = (uh * _silu(z[:, sl])).astype(o_ref.dtype)


def _gdn(p, ps, conv_w, norm_w, li, col0, row0, n_seq, t_len, latent, n_h, dk, s0=None):
    n_c = t_len // CHUNK
    hb, cpg = (1, 4) if n_c >= 4 else (4 // n_c, n_c)
    assert n_h % hb == 0 and n_c % cpg == 0
    wb = hb * dk
    rb = row0 // t_len
    cb = col0 // wb
    ng = n_h // hb
    in_specs = [pl.BlockSpec((t_len, wb), lambda b, h: (rb + b, cb + h)),
                pl.BlockSpec((t_len, wb), lambda b, h: (rb + b, cb + ng + h)),
                pl.BlockSpec((t_len, wb), lambda b, h: (rb + b, cb + 2 * ng + h)),
                pl.BlockSpec((t_len, wb), lambda b, h: (rb + b, cb + 3 * ng + h)),
                pl.BlockSpec((t_len, 128), lambda b, h: (rb + b, 0)),
                pl.BlockSpec((None, conv_w.shape[1], wb), lambda b, h: (li, 0, h)),
                pl.BlockSpec((None, conv_w.shape[1], wb), lambda b, h: (li, 0, ng + h)),
                pl.BlockSpec((None, conv_w.shape[1], wb), lambda b, h: (li, 0, 2 * ng + h)),
                pl.BlockSpec((None, 1, dk), lambda b, h: (li, 0, 0))]
    args = [p, p, p, p, ps, conv_w, conv_w, conv_w, norm_w]
    out_shape = [_sds((n_seq * t_len, n_h * dk), BF16)]
    out_specs = [pl.BlockSpec((t_len, wb), lambda b, h: (b, h))]
    if latent:
        in_specs.append(pl.BlockSpec((None, None, 2, hb, dk, dk), lambda b, h: (b, li, 0, h, 0, 0)))
        args.append(s0)
    else:
        out_shape.append(_sds((n_seq, 2, n_h, dk, dk), F32))
        out_specs.append(pl.BlockSpec((None, 2, hb, dk, dk), lambda b, h: (b, 0, h, 0, 0)))
    n_p = 2 * hb * n_c
    return pl.pallas_call(
        functools.partial(_gdn_body, t_len=t_len, latent=latent, n_h=n_h, dk=dk, hb=hb, cpg=cpg),
        out_shape=out_shape,
        grid=(n_seq, ng),
        in_specs=in_specs,
        out_specs=out_specs,
        scratch_shapes=[pltpu.VMEM((t_len, wb), F32)] * 4
        + [pltpu.VMEM((n_p, CHUNK, dk), F32), pltpu.VMEM((n_p, CHUNK, dk), BF16),
           pltpu.VMEM((n_p, CHUNK, CHUNK), BF16), pltpu.VMEM((n_p, CHUNK, dk), BF16),
           pltpu.VMEM((n_p, dk, CHUNK), BF16), pltpu.VMEM((n_p, 8, 128), F32)],
        compiler_params=_cp(("parallel", "parallel")),
        name="gdn_latent" if latent else "gdn_ctx",
    )(*args)


def _ctx_attn_body(q_ref, k_ref, v_ref, o_ref, *, n_h, dh):
    for h in range(n_h):
        sl = slice(h * dh, (h + 1) * dh)
        s = _bdot_nt(q_ref[:, sl] * (dh ** -0.5), k_ref[:, sl])
        m = jnp.max(s, axis=-1, keepdims=True)
        e = jnp.exp(s - m)
        o = _bdot(e, v_ref[:, sl]) / jnp.sum(e, axis=-1, keepdims=True)
        o_ref[:, sl] = o.astype(o_ref.dtype)


def _ctx_attn(p, n_seq, t_len, n_h, dh):
    d_c = n_h * dh
    return pl.pallas_call(
        functools.partial(_ctx_attn_body, n_h=n_h, dh=dh),
        out_shape=_sds((n_seq * t_len, d_c), BF16),
        grid=(n_seq,),
        in_specs=[pl.BlockSpec((t_len, d_c), lambda b: (b, 0)),
                  pl.BlockSpec((t_len, d_c), lambda b: (b, 1)),
                  pl.BlockSpec((t_len, d_c), lambda b: (b, 2))],
        out_specs=pl.BlockSpec((t_len, d_c), lambda b: (b, 0)),
        compiler_params=_cp(("parallel",)),
        name="ctx_attn",
    )(p, p, p)


def _na_body(q_ref, k_ref, v_ref, kc_ref, vc_ref, bias_ref, o_ref, *, rows, dh):
    kcb = kc_ref[...].astype(BF16)
    vcb = vc_ref[...].astype(BF16)

    rg = 4
    gw = rg * GRID_W

    def group_step(it, carry):
        q_all = q_ref[pl.ds(pl.multiple_of(it * gw, gw), gw), :] * (dh ** -0.5)
        s_ctx = _bdot_nt(q_all, kcb)
        r0s, s_wins = [], []
        for j in range(rg):
            r = it * rg + j
            r0 = jnp.clip(r - KH // 2, 0, rows - KH)
            kb = k_ref[pl.ds(pl.multiple_of(r0 * GRID_W, GRID_W), KH * GRID_W), :]
            s_wins.append(_bdot_nt(q_all[j * GRID_W:(j + 1) * GRID_W], kb) + bias_ref[r0 - r + KH - 1])
            r0s.append(r0)
        s_win = jnp.concatenate(s_wins, axis=0)
        m = jnp.maximum(jnp.max(s_win, axis=-1, keepdims=True), jnp.max(s_ctx, axis=-1, keepdims=True))
        e_win = jnp.exp(s_win - m)
        e_ctx = jnp.exp(s_ctx - m)
        den = jnp.sum(e_win, axis=-1, keepdims=True) + jnp.sum(e_ctx, axis=-1, keepdims=True)
        e_win = e_win.astype(BF16)
        o_ctx = _bdot(e_ctx, vcb)
        o_wins = []
        for j in range(rg):
            vb = v_ref[pl.ds(pl.multiple_of(r0s[j] * GRID_W, GRID_W), KH * GRID_W), :]
            o_wins.append(_bdot(e_win[j * GRID_W:(j + 1) * GRID_W], vb))
        o = (jnp.concatenate(o_wins, axis=0) + o_ctx) / den
        o_ref[pl.ds(pl.multiple_of(it * gw, gw), gw), :] = o.astype(o_ref.dtype)
        return carry

    lax.fori_loop(0, rows // rg, group_step, 0)


def _na_bias_table(rpb):
    q = np.arange(GRID_W)[:, None]
    kc = np.arange(GRID_W)[None, :]
    c0 = np.clip(q - KW // 2, 0, GRID_W - KW)
    valid = (kc >= c0) & (kc < c0 + KW)
    dcol = kc - q + KW - 1
    onehot = ((dcol[None] == np.arange(2 * KW - 1)[:, None, None]) & valid[None]).astype(np.float32)
    sel = jnp.einsum('lhrd,dqk->lhrqk', rpb.astype(F32), onehot, precision=lax.Precision.HIGHEST)
    tbl = jnp.where(valid[None, None, None], sel, NEG)
    cat = jnp.stack([tbl[:, :, d0:d0 + KH] for d0 in range(KH)], axis=2)
    n_l, n_h = rpb.shape[:2]
    return jnp.transpose(cat, (0, 1, 2, 4, 3, 5)).reshape(n_l, n_h, KH, GRID_W, KH * GRID_W)


def _na_attn(p, cache_k, cache_v, layer_i, bias, row0, n_seq, t_len, n_h, dh):
    rb = row0 // t_len
    past = cache_k.shape[2]
    rows = t_len // GRID_W
    return pl.pallas_call(
        functools.partial(_na_body, rows=rows, dh=dh),
        out_shape=_sds((n_seq * t_len, n_h * dh), BF16),
        grid=(n_seq, n_h),
        in_specs=[pl.BlockSpec((t_len, dh), lambda b, h: (rb + b, h)),
                  pl.BlockSpec((t_len, dh), lambda b, h: (rb + b, n_h + h)),
                  pl.BlockSpec((t_len, dh), lambda b, h: (rb + b, 2 * n_h + h)),
                  pl.BlockSpec((None, None, past, dh), lambda b, h: (b, layer_i, 0, h)),
                  pl.BlockSpec((None, None, past, dh), lambda b, h: (b, layer_i, 0, h)),
                  pl.BlockSpec((None, None, KH, GRID_W, KH * GRID_W), lambda b, h: (layer_i, h, 0, 0, 0))],
        out_specs=pl.BlockSpec((t_len, dh), lambda b, h: (b, h)),
        compiler_params=_cp(("parallel", "parallel")),
        name="na_attn",
    )(p, p, p, cache_k.reshape(cache_k.shape[:3] + (n_h * dh,)), cache_v.reshape(cache_v.shape[:3] + (n_h * dh,)), bias)


def _router_body(x_ref, g_ref, m_ref, wh_ref, wl_ref, b_ref, ri_ref, rw_ref, cnt_ref, base):
    @pl.when(pl.program_id(0) == 0)
    def _():
        base[...] = jnp.zeros_like(base)

    h = _norm_mod(x_ref[...], g_ref[...], m_ref, 3, 4)
    hh, hl = _split2(h)
    d = functools.partial(jnp.dot, preferred_element_type=F32)
    lg = d(hh, wh_ref[...]) + d(hh, wl_ref[...]) + d(hl, wh_ref[...]) + b_ref[...]
    tm = lg.shape[0]
    lane_i = lax.broadcasted_iota(jnp.int32, lg.shape, 1)
    lane = lane_i.astype(F32)
    big = 1e9
    gl = jnp.where(lane < N_GROUPS, lg, NEG)
    gmax = jnp.max(gl, axis=-1, keepdims=True)
    grp = jnp.min(jnp.where(gl == gmax, lane, big), axis=-1, keepdims=True)
    gate_g = 1.0 / jnp.sum(jnp.where(lane < N_GROUPS, jnp.exp(gl - gmax), 0.0), axis=-1, keepdims=True)
    lo = N_GROUPS + grp * EXP_PER_GROUP
    el = jnp.where((lane >= lo) & (lane < lo + EXP_PER_GROUP), lg, NEG)
    v1 = jnp.max(el, axis=-1, keepdims=True)
    i1 = jnp.min(jnp.where(el == v1, lane, big), axis=-1, keepdims=True)
    el2 = jnp.where(lane == i1, NEG, el)
    v2 = jnp.max(el2, axis=-1, keepdims=True)
    i2 = jnp.min(jnp.where(el2 == v2, lane, big), axis=-1, keepdims=True)
    e1 = i1 - N_GROUPS
    e2 = i2 - N_GROUPS
    t = jnp.exp(v2 - v1)
    w1 = gate_g / (1.0 + t)
    w2 = gate_g * t / (1.0 + t)
    oh = jnp.where((lane == e1) | (lane == e2), 1.0, 0.0)
    ii = lax.broadcasted_iota(jnp.int32, (tm, tm), 0)
    jj = lax.broadcasted_iota(jnp.int32, (tm, tm), 1)
    cnt = d(jnp.where(jj < ii, 1.0, 0.0).astype(BF16), oh.astype(BF16)) + base[...]
    rank1 = jnp.sum(jnp.where(lane == e1, cnt, 0.0), axis=-1, keepdims=True)
    rank2 = jnp.sum(jnp.where(lane == e2, cnt, 0.0), axis=-1, keepdims=True)
    info = jnp.where(lane == 0, e1, jnp.where(lane == 1, e2, jnp.where(lane == 2, rank1, jnp.where(lane == 3, rank2, 0.0))))
    ri_ref[...] = info.astype(jnp.int32)
    rw_ref[...] = jnp.where(lane == 0, w1, jnp.where(lane == 1, w2, 0.0))
    new_base = base[...] + jnp.sum(oh, axis=0, keepdims=True)
    base[...] = new_base
    cnt_ref[...] = new_base


def _router(x, g_all, gi, mod, layer, wr_hi, wr_lo, br, modidx, tm):
    n, d = x.shape
    return pl.pallas_call(
        _router_body,
        out_shape=[_sds((n, 128), jnp.int32), _sds((n, 128), F32), _sds((1, 128), F32)],
        grid=(n // tm,),
        in_specs=[pl.BlockSpec((tm, d), lambda i: (i, 0)),
                  pl.BlockSpec((None, 1, d), lambda i: (gi, 0, 0)),
                  _mod_spec(layer, modidx, d),
                  pl.BlockSpec((None, d, 128), lambda i: (layer, 0, 0)),
                  pl.BlockSpec((None, d, 128), lambda i: (layer, 0, 0)),
                  pl.BlockSpec((None, 1, 128), lambda i: (layer, 0, 0))],
        out_specs=[pl.BlockSpec((tm, 128), lambda i: (i, 0)),
                   pl.BlockSpec((tm, 128), lambda i: (i, 0)),
                   pl.BlockSpec((1, 128), lambda i: (0, 0))],
        scratch_shapes=[pltpu.VMEM((1, 128), F32)],
        compiler_params=_cp(("arbitrary",)),
        name="router",
    )(x, g_all, mod, wr_hi, wr_lo, br)


def _slab_rows(row, sub):
    start = row * sub
    return pl.ds(start if isinstance(start, int) else pl.multiple_of(start, sub), sub)


def _slab_copy(src, src_row, dst, dst_row, sub, sem):
    return pltpu.make_async_copy(src.at[_slab_rows(src_row, sub)], dst.at[_slab_rows(dst_row, sub)], sem)


def _slot_row(sp_ref, ri_ref, t, k):
    return sp_ref[ri_ref[4 * t + k]] + ri_ref[4 * t + 2 + k]


def _dispatch_body(sp_ref, ri_ref, x_ref, g_ref, m_ref, xp_in, xp_hbm, slab, sem, *, tm, sub, n_steps):
    del xp_in
    i = pl.program_id(0)
    slot = lax.rem(i, 2)

    def drain(s):
        def body(t, carry):
            for k in (0, 1):
                _slab_copy(slab.at[s], 0, xp_hbm, 0, sub, sem.at[s]).wait()
            return carry

        lax.fori_loop(0, tm, body, 0)

    @pl.when(i >= 2)
    def _():
        drain(slot)

    h = _norm_mod(x_ref[...], g_ref[...], m_ref, 3, 4)
    for c in range(sub):
        slab[slot, pl.ds(c, tm, stride=sub), :] = h[:, c * 128:(c + 1) * 128]

    def issue(t, carry):
        for k in (0, 1):
            _slab_copy(slab.at[slot], t, xp_hbm, _slot_row(sp_ref, ri_ref, t, k), sub, sem.at[slot]).start()
        return carry

    lax.fori_loop(0, tm, issue, 0)

    @pl.when(i == n_steps - 1)
    def _():
        if n_steps >= 2:
            drain(1 - slot)
        drain(slot)


def _dispatch(starts_p, ri_flat, x, g_all, gi, mod, layer, modidx, n_rows, tm):
    n, d = x.shape
    sub = d // 128
    xp0 = jnp.zeros((n_rows * sub, 128), F32)
    return pl.pallas_call(
        functools.partial(_dispatch_body, tm=tm, sub=sub, n_steps=n // tm),
        out_shape=_sds(xp0.shape, xp0.dtype),
        grid_spec=pltpu.PrefetchScalarGridSpec(
            num_scalar_prefetch=1,
            grid=(n // tm,),
            in_specs=[pl.BlockSpec((4 * tm,), lambda i, sp: (i,), memory_space=pltpu.SMEM),
                      pl.BlockSpec((tm, d), lambda i, sp: (i, 0)),
                      pl.BlockSpec((None, 1, d), lambda i, sp: (gi, 0, 0)),
                      _mod_spec(layer, modidx, d),
                      pl.BlockSpec(memory_space=pl.ANY)],
            out_specs=pl.BlockSpec(memory_space=pl.ANY),
            scratch_shapes=[pltpu.VMEM((2, tm * sub, 128), F32), pltpu.SemaphoreType.DMA((2,))]),
        input_output_aliases={5: 0},
        compiler_params=_cp(("arbitrary",)),
        name="moe_dispatch",
    )(starts_p, ri_flat, x, g_all, mod, xp0)


def _expert_body(be_ref, nu_ref, x_ref, w1_ref, w3_ref, w2_ref, o_ref, w1b, w3b, w2b, *, rows, sub):
    b = pl.program_id(0)
    used = b < nu_ref[0]

    @pl.when(used & ((b == 0) | (be_ref[b] != be_ref[jnp.maximum(b - 1, 0)])))
    def _():
        w1b[...] = w1_ref[...].astype(BF16)
        w3b[...] = w3_ref[...].astype(BF16)
        w2b[...] = w2_ref[...].astype(BF16)

    @pl.when(used)
    def _():
        xb = jnp.concatenate([x_ref[pl.ds(c, rows, stride=sub), :] for c in range(sub)], axis=1).astype(BF16)
        a = jnp.dot(xb, w1b[...], preferred_element_type=F32)
        g = jnp.dot(xb, w3b[...], preferred_element_type=F32)
        y = jnp.dot((_silu(a) * g).astype(BF16), w2b[...], preferred_element_type=F32)
        for c in range(sub):
            o_ref[pl.ds(c, rows, stride=sub), :] = y[:, c * 128:(c + 1) * 128]

    @pl.when(jnp.logical_not(used))
    def _():
        o_ref[...] = jnp.zeros_like(o_ref)


def _experts(blk_e, n_used, xp, w1, w3, w2, layer):
    d, de = w1.shape[2], w1.shape[3]
    sub = d // 128
    n_rows = xp.shape[0] // sub
    return pl.pallas_call(
        functools.partial(_expert_body, rows=MOE_ROWS, sub=sub),
        out_shape=_sds(xp.shape, F32),
        grid_spec=pltpu.PrefetchScalarGridSpec(
            num_scalar_prefetch=2,
            grid=(n_rows // MOE_ROWS,),
            in_specs=[pl.BlockSpec((MOE_ROWS * sub, 128), lambda b, be, nu: (b, 0)),
                      pl.BlockSpec((None, None, d, de), lambda b, be, nu: (layer, be[b], 0, 0)),
                      pl.BlockSpec((None, None, d, de), lambda b, be, nu: (layer, be[b], 0, 0)),
                      pl.BlockSpec((None, None, de, d), lambda b, be, nu: (layer, be[b], 0, 0))],
            out_specs=pl.BlockSpec((MOE_ROWS * sub, 128), lambda b, be, nu: (b, 0)),
            scratch_shapes=[pltpu.VMEM((d, de), BF16), pltpu.VMEM((d, de), BF16), pltpu.VMEM((de, d), BF16)]),
        compiler_params=_cp(("arbitrary",)),
        name="moe_experts",
    )(blk_e, n_used, xp, w1, w3, w2)


def _combine_body(sp_ref, ri_ref, rin_ref, x_ref, rw_ref, m_ref, yp_hbm, o_ref, slab, sem, *, tm, sub, n_steps):
    i = pl.program_id(0)
    slot = lax.rem(i, 2)

    def gather(idx_ref, s):
        def body(t, carry):
            for k in (0, 1):
                _slab_copy(yp_hbm, _slot_row(sp_ref, idx_ref, t, k), slab.at[s], 2 * t + k, sub, sem.at[s]).start()
            return carry

        lax.fori_loop(0, tm, body, 0)

    @pl.when(i == 0)
    def _():
        gather(ri_ref, 0)

    @pl.when(i + 1 < n_steps)
    def _():
        gather(rin_ref, 1 - slot)

    def drain(t, carry):
        for k in (0, 1):
            _slab_copy(yp_hbm, 0, slab.at[slot], 0, sub, sem.at[slot]).wait()
        return carry

    lax.fori_loop(0, tm, drain, 0)
    rw = rw_ref[...]
    w1, w2 = rw[:, 0:1], rw[:, 1:2]
    for c in range(sub):
        sl = slice(c * 128, (c + 1) * 128)
        y = slab[slot, pl.ds(c, tm, stride=2 * sub), :] * w1 + slab[slot, pl.ds(sub + c, tm, stride=2 * sub), :] * w2
        o_ref[:, sl] = x_ref[:, sl] + m_ref[5:6, sl] * y


def _combine(starts_p, ri_flat, x, rw, mod, layer, yp, modidx, tm):
    n, d = x.shape
    sub = d // 128
    n_steps = n // tm
    return pl.pallas_call(
        functools.partial(_combine_body, tm=tm, sub=sub, n_steps=n_steps),
        out_shape=_sds((n, d), F32),
        grid_spec=pltpu.PrefetchScalarGridSpec(
            num_scalar_prefetch=1,
            grid=(n_steps,),
            in_specs=[pl.BlockSpec((4 * tm,), lambda i, sp: (i,), memory_space=pltpu.SMEM),
                      pl.BlockSpec((4 * tm,), lambda i, sp: (jnp.minimum(i + 1, n_steps - 1),), memory_space=pltpu.SMEM),
                      pl.BlockSpec((tm, d), lambda i, sp: (i, 0)),
                      pl.BlockSpec((tm, 128), lambda i, sp: (i, 0)),
                      _mod_spec(layer, modidx, d),
                      pl.BlockSpec(memory_space=pl.ANY)],
            out_specs=pl.BlockSpec((tm, d), lambda i, sp: (i, 0)),
            scratch_shapes=[pltpu.VMEM((2, 2 * tm * sub, 128), F32), pltpu.SemaphoreType.DMA((2,))]),
        compiler_params=_cp(("arbitrary",)),
        name="moe_combine",
    )(starts_p, ri_flat, ri_flat, x, rw, mod, yp)


def _router_weights(wg, bg, we, be):
    pad = 128 - N_GROUPS - N_EXPERTS
    wr = jnp.pad(jnp.concatenate([wg, we], axis=2), ((0, 0), (0, 0), (0, pad)))
    wr_hi = wr.astype(BF16)
    wr_lo = (wr - wr_hi.astype(F32)).astype(BF16)
    br = jnp.pad(jnp.concatenate([bg, be], axis=1), ((0, 0), (0, pad)))[:, None, :]
    return wr_hi, wr_lo, br


def _moe(x, g_all, gi, mod, layer, wr_hi, wr_lo, br, w1, w3, w2, modidx):
    n, d = x.shape
    tm = 256
    ri, rw, cnt = _router(x, g_all, gi, mod, layer, wr_hi, wr_lo, br, modidx, tm)
    counts = cnt[0, :N_EXPERTS].astype(jnp.int32)
    padded = (counts + MOE_ROWS - 1) // MOE_ROWS * MOE_ROWS
    ends_p = jnp.cumsum(padded)
    starts_p = (ends_p - padded).astype(jnp.int32)
    n_blocks = (2 * n) // MOE_ROWS + N_EXPERTS
    blk_start = jnp.arange(n_blocks, dtype=jnp.int32) * MOE_ROWS
    blk_e = jnp.minimum(jnp.sum((ends_p[None, :] <= blk_start[:, None]).astype(jnp.int32), axis=1), N_EXPERTS - 1)
    n_used = (ends_p[-1:] // MOE_ROWS).astype(jnp.int32)
    ri_flat = ri[:, :4].reshape(-1)
    xp = _dispatch(starts_p, ri_flat, x, g_all, gi, mod, layer, modidx, n_blocks * MOE_ROWS, tm)
    yp = _experts(blk_e, n_used, xp, w1, w3, w2, layer)
    return _combine(starts_p, ri_flat, x, rw, mod, layer, yp, modidx, tm)


def _final_norm_body(x_ref, g_ref, o_ref):
    x = x_ref[...]
    o_ref[...] = x * lax.rsqrt(jnp.mean(x * x, axis=-1, keepdims=True) + EPS) * g_ref[...]


def _final_norm(x, g, row0, n_rows, tm):
    d = x.shape[1]
    rb = row0 // tm
    return pl.pallas_call(
        _final_norm_body,
        out_shape=_sds((n_rows, d), F32),
        grid=(n_rows // tm,),
        in_specs=[pl.BlockSpec((tm, d), lambda i: (rb + i, 0)), pl.BlockSpec((1, d), lambda i: (0, 0))],
        out_specs=pl.BlockSpec((tm, d), lambda i: (i, 0)),
        compiler_params=_cp(("parallel",)),
        name="final_norm",
    )(x, g.reshape(1, d))


def kernel(x_prompt, x_sample, c, state_ret, state_gdn, cache_k, cache_v, c_ctx, w_mod, b_mod, norm_g, even_w_in, even_w_out, ret_decay_logit, ret_gn_w, gdn_conv_w, gdn_a_log, gdn_dt_bias, gdn_norm_w, na_w_in, na_w_out, na_rpb, moe_wg, moe_bg, moe_we, moe_be, moe_w1, moe_w3, moe_w2, final_norm_g):
    bp, seq, d = x_prompt.shape
    bs, dseq, _ = x_sample.shape
    depth = w_mod.shape[0]
    h_a, dk_a = state_ret.shape[3], state_ret.shape[4]
    h_b, dk_b = state_gdn.shape[3], state_gdn.shape[4]
    h_c, dh_c = cache_k.shape[3], cache_k.shape[4]
    a_qk = h_a * dk_a
    b_qk = h_b * dk_b
    n_main = 4 * a_qk + 4 * b_qk
    np_rows = bp * seq
    n = np_rows + bs * dseq
    assert bs + 1 <= 8 and dseq % GRID_W == 0 and dseq // GRID_W >= KH and 4 * h_b <= 128
    assert np_rows % dseq == 0 and dseq % seq == 0

    tm_proj, tn_proj = 1024, 1024
    tm_out = 512
    modidx_proj = _mod_index(tm_proj, np_rows, dseq)
    modidx_out = _mod_index(tm_out, np_rows, dseq)
    modidx256 = _mod_index(256, np_rows, dseq)
    assert np_rows % tm_proj == 0 and dseq % tm_proj == 0

    cvec = jnp.concatenate([c_ctx[None, :], c, jnp.zeros((8 - 1 - bs, d), F32)], axis=0)
    mod = _adaln(cvec, w_mod, b_mod).reshape(depth, 8, 6, d)
    x = jnp.concatenate([x_prompt.reshape(np_rows, d), x_sample.reshape(bs * dseq, d)], axis=0)
    cos, sin = _rope_tables(dseq, dk_a)

    g_all = norm_g.reshape(depth * 2, 1, d)
    w_main = even_w_in[:, :, :n_main].astype(BF16)
    w_small = jnp.pad(even_w_in[:, :, n_main:], ((0, 0), (0, 0), (0, 128 - 4 * h_b))).astype(BF16)
    n_even = even_w_in.shape[0]
    gp = jnp.stack([jnp.pad(gdn_a_log.reshape(n_even, -1), ((0, 0), (0, 128 - 2 * h_b))),
                    jnp.pad(gdn_dt_bias.reshape(n_even, -1), ((0, 0), (0, 128 - 2 * h_b)))], axis=1).astype(F32)
    log_gamma = jax.nn.log_sigmoid(ret_decay_logit.astype(F32)).reshape(n_even * 2, h_a)
    gn_w = ret_gn_w.reshape(n_even, 1, h_a * dk_a)
    gdn_nw = gdn_norm_w.reshape(n_even, 1, dk_b)
    w_out_even = even_w_out.astype(BF16)
    w_in_odd = na_w_in.astype(BF16)
    w_out_odd = na_w_out.astype(BF16)
    bias = _na_bias_table(na_rpb)
    wr_hi, wr_lo, br = _router_weights(moe_wg, moe_bg, moe_we, moe_be)

    new_ret, new_gdn, new_k, new_v = [], [], [], []
    for l in range(depth):
        i = l // 2
        if l % 2 == 0:
            p, ps = _proj(x, g_all, 2 * l, mod, l, w_main, i, modidx_proj, tm_proj, tn_proj, small=(w_small, gp, i), n_h=h_b)
            a_p, sr = _retention(p, log_gamma, gn_w, i, 0, bp, seq, False, h_a, dk_a)
            (a_s,) = _retention(p, log_gamma, gn_w, i, np_rows, bs, dseq, True, h_a, dk_a, cos, sin, state_ret)
            b_p, sg = _gdn(p, ps, gdn_conv_w, gdn_nw, i, 4 * a_qk, 0, bp, seq, False, h_b, dk_b)
            (b_s,) = _gdn(p, ps, gdn_conv_w, gdn_nw, i, 4 * a_qk, np_rows, bs, dseq, True, h_b, dk_b, state_gdn)
            new_ret.append(sr)
            new_gdn.append(sg)
            x = _oproj([a_p, b_p], [a_s, b_s], w_out_even, i, x, mod, l, modidx_out, 2, tm_out, d)
        else:
            d_c = h_c * dh_c
            p = _proj(x, g_all, 2 * l, mod, l, w_in_odd, i, modidx_proj, tm_proj, tn_proj)
            o_p = _ctx_attn(p, bp, seq, h_c, dh_c)
            o_s = _na_attn(p, cache_k, cache_v, i, bias, np_rows, bs, dseq, h_c, dh_c)
            new_k.append(p[:np_rows, d_c:2 * d_c].reshape(bp, seq, h_c, dh_c))
            new_v.append(p[:np_rows, 2 * d_c:].reshape(bp, seq, h_c, dh_c))
            x = _oproj([o_p], [o_s], w_out_odd, i, x, mod, l, modidx_out, 2, tm_out, d)
        x = _moe(x, g_all, 2 * l + 1, mod, l, wr_hi, wr_lo, br, moe_w1, moe_w3, moe_w2, modidx256)

    y_p = _final_norm(x, final_norm_g, 0, np_rows, tm_out)
    y_s = _final_norm(x, final_norm_g, np_rows, bs * dseq, tm_out)
    return (y_p.reshape(bp, seq, d), y_s.reshape(bs, dseq, d),
            jnp.stack(new_ret, axis=1), jnp.stack(new_gdn, axis=1), jnp.stack(new_k, axis=1), jnp.stack(new_v, axis=1))
```
